```python
import jax, jax.numpy as jnp
from jax import lax
import numpy as np

D_MODEL = 2048
BATCH = 1
SEQ = 8192
DEPTH = 2
DEC_BATCH = 2
DEC_SEQ = 16384
PAST_LEN = 128

HEAD_DIM = 64
A_WIDTH = D_MODEL // 2
H_A = A_WIDTH // HEAD_DIM
B_WIDTH = D_MODEL - A_WIDTH
IN_PROJ = 3 * A_WIDTH + 2 * B_WIDTH
GRID_W = 64
WIN_ROWS = 8
WIN_COLS = 16
CONV_K = 31
FOURIER_GROUPS = 4
D_FF = 256 * ((8 * D_MODEL // 3 + 255) // 256)
NORM_EPS = 1e-6

kernel_name = "hybrid_natten_conformer_fnet_encoder"


def rmsnorm(x, g):
    x32 = x.astype(jnp.float32)
    y = x32 * lax.rsqrt(jnp.mean(x32 * x32, axis=-1, keepdims=True) + NORM_EPS)
    return (y * g.astype(jnp.float32)).astype(x.dtype)


def layernorm(x, g, b):
    x32 = x.astype(jnp.float32)
    mu = jnp.mean(x32, axis=-1, keepdims=True)
    xc = x32 - mu
    var = jnp.mean(xc * xc, axis=-1, keepdims=True)
    y = xc * lax.rsqrt(var + NORM_EPS)
    return (y * g.astype(jnp.float32) + b.astype(jnp.float32)).astype(x.dtype)


def swiglu(h, w_gate, w_up, w_down):
    return (jax.nn.silu(h @ w_gate) * (h @ w_up)) @ w_down


def neighborhood_attention(q, k, v, rpb):
    b, s, h, dh = q.shape
    rows = s // GRID_W
    kh = min(WIN_ROWS, rows)
    q = q.reshape(b, rows, GRID_W, h, dh)
    k = k.reshape(b, rows, GRID_W, h, dh)
    v = v.reshape(b, rows, GRID_W, h, dh)
    r_ids = jnp.arange(rows)
    row_start = jnp.clip(r_ids - kh // 2, 0, rows - kh)
    c_ids = jnp.arange(GRID_W)
    col_start = jnp.clip(c_ids - WIN_COLS // 2, 0, GRID_W - WIN_COLS)
    col_idx = col_start[:, None] + jnp.arange(WIN_COLS)[None, :]
    dc = col_idx - c_ids[:, None]
    scale = dh ** -0.5

    def one_row(r):
        r0 = row_start[r]
        k_rows = lax.dynamic_slice_in_dim(k, r0, kh, axis=1)
        v_rows = lax.dynamic_slice_in_dim(v, r0, kh, axis=1)
        k_g = k_rows[:, :, col_idx]
        v_g = v_rows[:, :, col_idx]
        q_r = lax.dynamic_index_in_dim(q, r, axis=1, keepdims=False)
        dr = r0 + jnp.arange(kh) - r
        bias = rpb[:, dr[:, None, None] + WIN_ROWS - 1, dc[None, :, :] + WIN_COLS - 1]
        bias = jnp.transpose(bias, (0, 2, 1, 3)).astype(jnp.float32)
        sc = jnp.einsum('bqhd,bjqkhd->bhqjk', q_r, k_g).astype(jnp.float32) * scale + bias[None]
        p = jax.nn.softmax(sc.reshape(b, h, GRID_W, kh * WIN_COLS), axis=-1)
        p = p.reshape(sc.shape).astype(v.dtype)
        return jnp.einsum('bhqjk,bjqkhd->bqhd', p, v_g)

    out = lax.map(one_row, r_ids)
    return jnp.transpose(out, (1, 0, 2, 3, 4)).reshape(b, s, h * dh)


def depthwise_conv(x, w, bias):
    c = x.shape[-1]
    y = lax.conv_general_dilated(
        x, w[:, None, :].astype(x.dtype), window_strides=(1,),
        padding=[(CONV_K // 2, CONV_K // 2)],
        dimension_numbers=('NWC', 'WIO', 'NWC'), feature_group_count=c)
    return y + bias.astype(x.dtype)


def attn_conv_mixer(h, w_in, rpb, conv_w, conv_b, ln_g, ln_b, w_out):
    b, s, _ = h.shape
    proj = h @ w_in
    q, k, v, a, g = jnp.split(proj, [A_WIDTH, 2 * A_WIDTH, 3 * A_WIDTH, 3 * A_WIDTH + B_WIDTH], axis=-1)
    q = q.reshape(b, s, H_A, HEAD_DIM)
    k = k.reshape(b, s, H_A, HEAD_DIM)
    v = v.reshape(b, s, H_A, HEAD_DIM)
    att = neighborhood_attention(q, k, v, rpb)
    c = a * jax.nn.sigmoid(g)
    c = depthwise_conv(c, conv_w, conv_b)
    c = jax.nn.silu(layernorm(c, ln_g, ln_b))
    return jnp.concatenate([att, c], axis=-1) @ w_out


def fourier_mixer(h, w_out):
    b, s, d = h.shape
    hf = h.astype(jnp.float32).reshape(b, s, FOURIER_GROUPS, d // FOURIER_GROUPS)
    y = jnp.fft.fft2(hf, axes=(1, 3), norm='ortho').real
    return y.reshape(b, s, d).astype(h.dtype) @ w_out


def trunk(x, ffn1_norm, ffn1_w_gate, ffn1_w_up, ffn1_w_down, mix_norm,
          ab_w_in, ab_rpb, ab_conv_w, ab_conv_b, ab_ln_g, ab_ln_b, ab_w_out,
          c_w_out, ffn2_norm, ffn2_w_gate, ffn2_w_up, ffn2_w_down, final_norm):
    for i in range(DEPTH):
        x = x + 0.5 * swiglu(rmsnorm(x, ffn1_norm[i]), ffn1_w_gate[i], ffn1_w_up[i], ffn1_w_down[i])
        h = rmsnorm(x, mix_norm[i])
        j = i // 2
        if i % 2 == 0:
            x = x + attn_conv_mixer(h, ab_w_in[j], ab_rpb[j], ab_conv_w[j], ab_conv_b[j],
                                    ab_ln_g[j], ab_ln_b[j], ab_w_out[j])
        else:
            x = x + fourier_mixer(h, c_w_out[j])
        x = x + 0.5 * swiglu(rmsnorm(x, ffn2_norm[i]), ffn2_w_gate[i], ffn2_w_up[i], ffn2_w_down[i])
    return rmsnorm(x, final_norm)


def setup_inputs(seed: int = 0) -> dict:
    key = jax.random.key(seed)
    ks = jax.random.split(key, 24)
    n_even = (DEPTH + 1) // 2
    n_odd = DEPTH // 2
    f32 = jnp.float32

    def dense(k, shape, fan_in):
        return jax.random.normal(k, shape, f32) * (fan_in ** -0.5)

    def gain(k, shape):
        return 1.0 + 0.01 * jax.random.normal(k, shape, f32)

    def small(k, shape):
        return 0.01 * jax.random.normal(k, shape, f32)

    return {
        "x_prompt": jax.random.normal(ks[0], (BATCH, SEQ, D_MODEL), f32),
        "x_sample": jax.random.normal(ks[1], (DEC_BATCH, DEC_SEQ, D_MODEL), f32),
        "ffn1_norm": gain(ks[2], (DEPTH, D_MODEL)),
        "ffn1_w_gate": dense(ks[3], (DEPTH, D_MODEL, D_FF), D_MODEL),
        "ffn1_w_up": dense(ks[4], (DEPTH, D_MODEL, D_FF), D_MODEL),
        "ffn1_w_down": dense(ks[5], (DEPTH, D_FF, D_MODEL), D_FF),
        "mix_norm": gain(ks[6], (DEPTH, D_MODEL)),
        "ab_w_in": dense(ks[7], (n_even, D_MODEL, IN_PROJ), D_MODEL),
        "ab_rpb": 0.02 * jax.random.normal(ks[8], (n_even, H_A, 2 * WIN_ROWS - 1, 2 * WIN_COLS - 1), f32),
        "ab_conv_w": dense(ks[9], (n_even, CONV_K, B_WIDTH), CONV_K),
        "ab_conv_b": small(ks[10], (n_even, B_WIDTH)),
        "ab_ln_g": gain(ks[11], (n_even, B_WIDTH)),
        "ab_ln_b": small(ks[12], (n_even, B_WIDTH)),
        "ab_w_out": dense(ks[13], (n_even, D_MODEL, D_MODEL), D_MODEL),
        "c_w_out": dense(ks[14], (n_odd, D_MODEL, D_MODEL), D_MODEL),
        "ffn2_norm": gain(ks[15], (DEPTH, D_MODEL)),
        "ffn2_w_gate": dense(ks[16], (DEPTH, D_MODEL, D_FF), D_MODEL),
        "ffn2_w_up": dense(ks[17], (DEPTH, D_MODEL, D_FF), D_MODEL),
        "ffn2_w_down": dense(ks[18], (DEPTH, D_FF, D_MODEL), D_FF),
        "final_norm": gain(ks[19], (D_MODEL,)),
    }


def reference(x_prompt, x_sample, ffn1_norm, ffn1_w_gate, ffn1_w_up, ffn1_w_down, mix_norm,
              ab_w_in, ab_rpb, ab_conv_w, ab_conv_b, ab_ln_g, ab_ln_b, ab_w_out,
              c_w_out, ffn2_norm, ffn2_w_gate, ffn2_w_up, ffn2_w_down, final_norm):
    y_prompt = trunk(x_prompt, ffn1_norm, ffn1_w_gate, ffn1_w_up, ffn1_w_down, mix_norm,
                     ab_w_in, ab_rpb, ab_conv_w, ab_conv_b, ab_ln_g, ab_ln_b, ab_w_out,
                     c_w_out, ffn2_norm, ffn2_w_gate, ffn2_w_up, ffn2_w_down, final_norm)
    y_sample = trunk(x_sample, ffn1_norm, ffn1_w_gate, ffn1_w_up, ffn1_w_down, mix_norm,
                     ab_w_in, ab_rpb, ab_conv_w, ab_conv_b, ab_ln_g, ab_ln_b, ab_w_out,
                     c_w_out, ffn2_norm, ffn2_w_gate, ffn2_w_up, ffn2_w_down, final_norm)
    return (y_prompt, y_sample)
```

```python
import functools
import math

import numpy as np
import jax
import jax.numpy as jnp
from jax import lax
from jax.experimental import pallas as pl
from jax.experimental.pallas import tpu as pltpu

F32 = jnp.float32
BF16 = jnp.bfloat16

HEAD_DIM = 64
GRID_W = 64
WIN_ROWS = 8
WIN_COLS = 16
CONV_K = 31
FOURIER_GROUPS = 4
NORM_EPS = 1e-6

LANES = 128
BF16_SUBLANES = 16
VMEM_BUDGET_BYTES = 60000 * 1024

Q_ROWS = 4
K_ROWS = 12
Q_TOK = Q_ROWS * GRID_W
K_TOK = K_ROWS * GRID_W
MASKED = -1e30
DFT_INNER = 128
CONV_HALO = 16


def _vmem_limit(pipelined_bytes, resident_bytes=0):
    return int(min(VMEM_BUDGET_BYTES, 2 * pipelined_bytes + resident_bytes + (4 << 20)))


def _nbytes(shape, dtype):
    return math.prod(shape) * jnp.dtype(dtype).itemsize


def _params(semantics, pipelined_bytes, resident_bytes=0):
    return pltpu.CompilerParams(dimension_semantics=semantics,
                                vmem_limit_bytes=_vmem_limit(pipelined_bytes, resident_bytes))


def _rms(x, gain):
    return x * lax.rsqrt(jnp.mean(x * x, axis=-1, keepdims=True) + NORM_EPS) * gain


def _tile(n, preferred):
    best = LANES
    for c in range(LANES, min(n, preferred) + 1, LANES):
        if n % c == 0:
            best = c
    assert n % best == 0
    return best


def _seg_select(i, bounds, values):
    out = values[-1]
    for s in range(len(values) - 2, -1, -1):
        out = jnp.where(i < bounds[s + 1], values[s], out)
    return out


def _ffn_kernel(x_ref, g_ref, wg_ref, wu_ref, wd_ref, o_ref, h_ref):
    @pl.when(pl.program_id(1) == 0)
    def _():
        x = x_ref[...]
        h_ref[...] = _rms(x, g_ref[...]).astype(BF16)
        o_ref[...] = x

    h = h_ref[...]
    gate = jnp.dot(h, wg_ref[...], preferred_element_type=F32)
    up = jnp.dot(h, wu_ref[...], preferred_element_type=F32)
    act = (gate * jax.nn.sigmoid(gate)) * (0.5 * up)
    o_ref[...] += jnp.dot(act.astype(BF16), wd_ref[...], preferred_element_type=F32)


def _ffn(x, gain, w_gate, w_up, w_down, *, tm, tf):
    t, d = x.shape
    f = w_gate.shape[1]
    pipelined = (_nbytes((tm, d), F32) * 2 + 2 * _nbytes((d, tf), BF16) + _nbytes((tf, d), BF16))
    resident = _nbytes((tm, d), BF16) + 3 * _nbytes((tm, tf), F32)
    return pl.pallas_call(
        _ffn_kernel,
        out_shape=jax.ShapeDtypeStruct((t, d), F32),
        grid=(t // tm, f // tf),
        in_specs=[
            pl.BlockSpec((tm, d), lambda i, j: (i, 0)),
            pl.BlockSpec((1, d), lambda i, j: (0, 0)),
            pl.BlockSpec((d, tf), lambda i, j: (0, j)),
            pl.BlockSpec((d, tf), lambda i, j: (0, j)),
            pl.BlockSpec((tf, d), lambda i, j: (j, 0)),
        ],
        out_specs=pl.BlockSpec((tm, d), lambda i, j: (i, 0)),
        scratch_shapes=[pltpu.VMEM((tm, d), BF16)],
        compiler_params=_params(("parallel", "arbitrary"), pipelined, resident),
        name="ffn",
    )(x, gain.reshape(1, d), w_gate, w_up, w_down)


def _qkv_kernel(x_ref, g_ref, w_ref, o_ref, h_ref):
    @pl.when(pl.program_id(1) == 0)
    def _():
        h_ref[...] = _rms(x_ref[...], g_ref[...]).astype(BF16)

    r = jnp.dot(h_ref[...], w_ref[...], preferred_element_type=F32)
    for p in range(o_ref.shape[0]):
        o_ref[p] = r[:, p * LANES:(p + 1) * LANES].astype(BF16)


def _qkv(x, gain, w_in, n_cols, *, tm, tn):
    t, d = x.shape
    pipelined = _nbytes((tm, d), F32) + _nbytes((d, tn), BF16) + _nbytes((tm, tn), BF16)
    resident = _nbytes((tm, d), BF16) + _nbytes((tm, tn), F32)
    return pl.pallas_call(
        _qkv_kernel,
        out_shape=jax.ShapeDtypeStruct((n_cols // LANES, t, LANES), BF16),
        grid=(t // tm, n_cols // tn),
        in_specs=[
            pl.BlockSpec((tm, d), lambda i, j: (i, 0)),
            pl.BlockSpec((1, d), lambda i, j: (0, 0)),
            pl.BlockSpec((d, tn), lambda i, j: (0, j)),
        ],
        out_specs=pl.BlockSpec((tn // LANES, tm, LANES), lambda i, j: (j, i, 0)),
        scratch_shapes=[pltpu.VMEM((tm, d), BF16)],
        compiler_params=_params(("parallel", "arbitrary"), pipelined, resident),
        name="qkv_proj",
    )(x, gain.reshape(1, d), w_in)


def _glu_kernel(x_ref, g_ref, wa_ref, wg_ref, o_ref, h_ref):
    @pl.when(pl.program_id(1) == 0)
    def _():
        h_ref[...] = _rms(x_ref[...], g_ref[...]).astype(BF16)

    h = h_ref[...]
    a = jnp.dot(h, wa_ref[...], preferred_element_type=F32)
    g = jnp.dot(h, wg_ref[...], preferred_element_type=F32)
    o_ref[...] = a * jax.nn.sigmoid(g)


def _glu(x, gain, w_in, a_col0, g_col0, width, *, tm, tn):
    t, d = x.shape
    pipelined = _nbytes((tm, d), F32) + 2 * _nbytes((d, tn), BF16) + _nbytes((tm, tn), F32)
    resident = _nbytes((tm, d), BF16) + 2 * _nbytes((tm, tn), F32)
    return pl.pallas_call(
        _glu_kernel,
        out_shape=jax.ShapeDtypeStruct((t, width), F32),
        grid=(t // tm, width // tn),
        in_specs=[
            pl.BlockSpec((tm, d), lambda i, j: (i, 0)),
            pl.BlockSpec((1, d), lambda i, j: (0, 0)),
            pl.BlockSpec((d, tn), lambda i, j: (0, a_col0 // tn + j)),
            pl.BlockSpec((d, tn), lambda i, j: (0, g_col0 // tn + j)),
        ],
        out_specs=pl.BlockSpec((tm, tn), lambda i, j: (i, j)),
        scratch_shapes=[pltpu.VMEM((tm, d), BF16)],
        compiler_params=_params(("parallel", "arbitrary"), pipelined, resident),
        name="glu_proj",
    )(x, gain.reshape(1, d), w_in, w_in)


def _attn_bias_tables(rpb):
    n_heads = rpb.shape[0]
    qr = np.arange(Q_ROWS)[:, None]
    j = np.arange(K_ROWS)[None, :]
    dr, r_ok = [], []
    for case in range(3):
        shift = case * Q_ROWS
        first = [np.zeros_like(qr), qr, np.full_like(qr, Q_ROWS)][case]
        dr.append(j - shift - qr)
        r_ok.append((j >= first) & (j < first + WIN_ROWS))
    dr = np.stack(dr) + (WIN_ROWS - 1)
    r_ok = np.stack(r_ok)
    c = np.arange(GRID_W)[:, None]
    kc = np.arange(GRID_W)[None, :]
    c0 = np.clip(c - WIN_COLS // 2, 0, GRID_W - WIN_COLS)
    c_ok = (kc >= c0) & (kc < c0 + WIN_COLS)
    dc = kc - c + (WIN_COLS - 1)
    ok = r_ok[:, :, None, :, None] & c_ok[None, None, :, None, :]
    dr_i = np.clip(dr, 0, 2 * WIN_ROWS - 2)[:, :, None, :, None]
    dc_i = np.clip(dc, 0, 2 * WIN_COLS - 2)[None, None, :, None, :]
    dr_i, dc_i = np.broadcast_arrays(dr_i, dc_i)
    vals = rpb.astype(F32)[:, dr_i, dc_i]
    vals = jnp.where(ok[None], vals, MASKED)
    vals = jnp.transpose(vals, (1, 0, 2, 3, 4, 5))
    return vals.reshape(3, n_heads // 2, 2 * Q_TOK, K_TOK)


def _attn_kernel(q_ref, k0_ref, k1_ref, k2_ref, v0_ref, v1_ref, v2_ref, b_ref, o_ref):
    q = q_ref[0]
    lane = lax.broadcasted_iota(jnp.int32, q.shape, 1)
    zero = jnp.zeros_like(q)
    q2 = jnp.concatenate([jnp.where(lane < HEAD_DIM, q, zero), jnp.where(lane >= HEAD_DIM, q, zero)], axis=0)
    k = jnp.concatenate([k0_ref[0], k1_ref[0], k2_ref[0]], axis=0)
    v = jnp.concatenate([v0_ref[0], v1_ref[0], v2_ref[0]], axis=0)
    s = lax.dot_general(q2, k, (((1,), (1,)), ((), ())), preferred_element_type=F32)
    s = s * (HEAD_DIM ** -0.5) + b_ref[0, 0]
    e = jnp.exp(s - jnp.max(s, axis=-1, keepdims=True))
    denom = jnp.sum(e, axis=-1, keepdims=True)
    o2 = jnp.dot(e.astype(BF16), v, preferred_element_type=F32) / denom
    o_ref[0] = jnp.where(lane < HEAD_DIM, o2[:Q_TOK], o2[Q_TOK:]).astype(BF16)


def _attention(qkv, bias, seg_groups):
    n_pairs = qkv.shape[0] // 3
    t = qkv.shape[1]
    lo_tab = seg_groups[:-1]
    hi_tab = seg_groups[1:]

    def window(i):
        lo = _seg_select(i, seg_groups, lo_tab)
        hi = _seg_select(i, seg_groups, hi_tab)
        return jnp.clip(i - 1, lo, hi - K_ROWS // Q_ROWS), lo, hi

    def case(i):
        _, lo, hi = window(i)
        return jnp.where(i == lo, 0, jnp.where(i == hi - 1, 2, 1))

    blk = (1, Q_TOK, LANES)
    kv_specs = [pl.BlockSpec(blk, functools.partial(lambda p, i, off, j: (off + p, window(i)[0] + j, 0), off=off, j=j))
                for off in (n_pairs, 2 * n_pairs) for j in range(K_ROWS // Q_ROWS)]
    pipelined = 8 * _nbytes(blk, BF16) + _nbytes((2 * Q_TOK, K_TOK), F32)
    resident = 4 * _nbytes((2 * Q_TOK, K_TOK), F32)
    return pl.pallas_call(
        _attn_kernel,
        out_shape=jax.ShapeDtypeStruct((n_pairs, t, LANES), BF16),
        grid=(n_pairs, t // Q_TOK),
        in_specs=[pl.BlockSpec(blk, lambda p, i: (p, i, 0))] + kv_specs + [
            pl.BlockSpec((1, 1, 2 * Q_TOK, K_TOK), lambda p, i: (case(i), p, 0, 0))],
        out_specs=pl.BlockSpec(blk, lambda p, i: (p, i, 0)),
        compiler_params=_params(("parallel", "arbitrary"), pipelined, resident),
        name="nbr_attention",
    )(qkv, *([qkv] * 6), bias)


def _conv_kernel(first_tiles, last_tiles, prev_ref, cur_ref, next_ref, w_ref, cb_ref, lg_ref, lb_ref,
                 o_ref, buf_ref, y_ref):
    i = pl.program_id(0)
    tm, c = cur_ref.shape
    is_first = functools.reduce(jnp.logical_or, [i == s for s in first_tiles])
    is_last = functools.reduce(jnp.logical_or, [i == s for s in last_tiles])
    buf_ref[0:CONV_HALO, :] = jnp.where(is_first, 0.0, prev_ref[...])
    buf_ref[CONV_HALO:CONV_HALO + tm, :] = cur_ref[...]
    buf_ref[CONV_HALO + tm:, :] = jnp.where(is_last, 0.0, next_ref[...])
    chunk = 64
    base = CONV_HALO - CONV_K // 2
    for t0 in range(0, tm, chunk):
        for l0 in range(0, c, LANES):
            acc = jnp.broadcast_to(cb_ref[:, l0:l0 + LANES], (chunk, LANES))
            for k in range(CONV_K):
                acc = acc + w_ref[k:k + 1, l0:l0 + LANES] * buf_ref[t0 + base + k:t0 + base + k + chunk, l0:l0 + LANES]
            y_ref[t0:t0 + chunk, l0:l0 + LANES] = acc
    y = y_ref[...]
    mu = jnp.mean(y, axis=-1, keepdims=True)
    yc = y - mu
    var = jnp.mean(yc * yc, axis=-1, keepdims=True)
    z = yc * lax.rsqrt(var + NORM_EPS) * lg_ref[...] + lb_ref[...]
    o_ref[...] = (z * jax.nn.sigmoid(z)).astype(BF16)


def _conv_module(c, conv_w, conv_b, ln_g, ln_b, seq_starts, seq_ends, *, tm):
    t, width = c.shape
    first_tiles = tuple(s // tm for s in seq_starts)
    last_tiles = tuple(e // tm - 1 for e in seq_ends)
    hb = tm // CONV_HALO
    n_hb = t // CONV_HALO
    pipelined = _nbytes((tm + 2 * CONV_HALO, width), F32) + _nbytes((tm, width), BF16)
    resident = _nbytes((2 * tm + 2 * CONV_HALO, width), F32) + 3 * _nbytes((tm, width), F32)
    row = lambda a: a.reshape(1, width)
    return pl.pallas_call(
        functools.partial(_conv_kernel, first_tiles, last_tiles),
        out_shape=jax.ShapeDtypeStruct((t, width), BF16),
        grid=(t // tm,),
        in_specs=[
            pl.BlockSpec((CONV_HALO, width), lambda i: (jnp.maximum(i * hb - 1, 0), 0)),
            pl.BlockSpec((tm, width), lambda i: (i, 0)),
            pl.BlockSpec((CONV_HALO, width), lambda i: (jnp.minimum((i + 1) * hb, n_hb - 1), 0)),
            pl.BlockSpec((CONV_K, width), lambda i: (0, 0)),
            pl.BlockSpec((1, width), lambda i: (0, 0)),
            pl.BlockSpec((1, width), lambda i: (0, 0)),
            pl.BlockSpec((1, width), lambda i: (0, 0)),
        ],
        out_specs=pl.BlockSpec((tm, width), lambda i: (i, 0)),
        scratch_shapes=[pltpu.VMEM((tm + 2 * CONV_HALO, width), F32), pltpu.VMEM((tm, width), F32)],
        compiler_params=_params(("parallel",), pipelined, resident),
        name="conv_module",
    )(c, c, c, conv_w, row(conv_b), row(ln_g), row(ln_b))


def _outproj_kernel(x_ref, att_ref, cc_ref, w1_ref, w2_ref, o_ref):
    att = jnp.concatenate([att_ref[p] for p in range(att_ref.shape[0])], axis=1)
    acc = jnp.dot(att, w1_ref[...], preferred_element_type=F32)
    acc = acc + jnp.dot(cc_ref[...], w2_ref[...], preferred_element_type=F32)
    o_ref[...] = x_ref[...] + acc


def _outproj(x, att, cc, w_out, *, tm, tn):
    t, d = x.shape
    n_pairs = att.shape[0]
    wa = n_pairs * LANES
    wb = cc.shape[1]
    assert wa == wb and wa + wb == w_out.shape[0]
    pipelined = (2 * _nbytes((tm, tn), F32) + _nbytes((tm, wa), BF16) + _nbytes((tm, wb), BF16)
                 + _nbytes((wa + wb, tn), BF16))
    resident = 2 * _nbytes((tm, tn), F32) + _nbytes((tm, wa), BF16)
    return pl.pallas_call(
        _outproj_kernel,
        out_shape=jax.ShapeDtypeStruct((t, d), F32),
        grid=(t // tm, d // tn),
        in_specs=[
            pl.BlockSpec((tm, tn), lambda i, j: (i, j)),
            pl.BlockSpec((n_pairs, tm, LANES), lambda i, j: (0, i, 0)),
            pl.BlockSpec((tm, wb), lambda i, j: (i, 0)),
            pl.BlockSpec((wa, tn), lambda i, j: (0, j)),
            pl.BlockSpec((wb, tn), lambda i, j: (1, j)),
        ],
        out_specs=pl.BlockSpec((tm, tn), lambda i, j: (i, j)),
        compiler_params=_params(("parallel", "arbitrary"), pipelined, resident),
        name="attn_conv_out_proj",
    )(x, att, cc, w_out, w_out)


def _resmm_kernel(x_ref, a_ref, w_ref, o_ref):
    o_ref[...] = x_ref[...] + jnp.dot(a_ref[...], w_ref[...], preferred_element_type=F32)


def _resmm(x, a, w, *, tm, tn):
    t, d = x.shape
    kdim = a.shape[1]
    pipelined = 2 * _nbytes((tm, tn), F32) + _nbytes((tm, kdim), BF16) + _nbytes((kdim, tn), BF16)
    resident = 2 * _nbytes((tm, tn), F32)
    return pl.pallas_call(
        _resmm_kernel,
        out_shape=jax.ShapeDtypeStruct((t, d), F32),
        grid=(t // tm, d // tn),
        in_specs=[
            pl.BlockSpec((tm, tn), lambda i, j: (i, j)),
            pl.BlockSpec((tm, kdim), lambda i, j: (i, 0)),
            pl.BlockSpec((kdim, tn), lambda i, j: (0, j)),
        ],
        out_specs=pl.BlockSpec((tm, tn), lambda i, j: (i, j)),
        compiler_params=_params(("parallel", "arbitrary"), pipelined, resident),
        name="fourier_out_proj",
    )(x, a, w)


def _cos_sin(num, den):
    ang = (2.0 * np.pi / den) * (num % den).astype(np.float64)
    return np.cos(ang).astype(np.float32), np.sin(ang).astype(np.float32)


def _channel_dft_matrix(n):
    idx = np.arange(n)
    c, s = _cos_sin(idx[:, None] * idx[None, :], n)
    return np.concatenate([c, -s], axis=1)


def _outer_dft_matrix(s1):
    idx = np.arange(s1)
    c, s = _cos_sin(idx[:, None] * idx[None, :], s1)
    return np.block([[c, s], [-s, c]])


def _inner_dft_tables(s1):
    s = s1 * DFT_INNER
    k1 = jnp.arange(s1, dtype=jnp.int32)[:, None, None]
    k2 = jnp.arange(DFT_INNER, dtype=jnp.int32)[None, :, None]
    n2 = jnp.arange(DFT_INNER, dtype=jnp.int32)[None, None, :]
    num = (n2 * (k1 + s1 * k2)) % s
    ang = num.astype(F32) * F32(2.0 * np.pi / s)
    return jnp.concatenate([jnp.cos(ang), jnp.sin(ang)], axis=2)


def _chdft_kernel(x_ref, g_ref, d_ref, ar_ref, ai_ref):
    h = _rms(x_ref[...], g_ref[...]).astype(BF16)
    gw = d_ref.shape[0]
    for q in range(h.shape[1] // gw):
        r = jnp.dot(h[:, q * gw:(q + 1) * gw], d_ref[...], preferred_element_type=F32)
        ar_ref[:, q * gw:(q + 1) * gw] = r[:, :gw].astype(BF16)
        ai_ref[:, q * gw:(q + 1) * gw] = r[:, gw:].astype(BF16)


def _channel_dft(x, gain, dmat, *, tm):
    t, d = x.shape
    gw = dmat.shape[0]
    pipelined = _nbytes((tm, d), F32) + 2 * _nbytes((tm, d), BF16) + _nbytes(dmat.shape, BF16)
    resident = _nbytes((tm, d), BF16) + 2 * _nbytes((tm, 2 * gw), F32) + _nbytes((tm, d), F32)
    out = jax.ShapeDtypeStruct((t, d), BF16)
    return pl.pallas_call(
        _chdft_kernel,
        out_shape=(out, out),
        grid=(t // tm,),
        in_specs=[
            pl.BlockSpec((tm, d), lambda i: (i, 0)),
            pl.BlockSpec((1, d), lambda i: (0, 0)),
            pl.BlockSpec(dmat.shape, lambda i: (0, 0)),
        ],
        out_specs=(pl.BlockSpec((tm, d), lambda i: (i, 0)), pl.BlockSpec((tm, d), lambda i: (i, 0))),
        compiler_params=_params(("parallel",), pipelined, resident),
        name="channel_dft",
    )(x, gain.reshape(1, d), dmat)


def _dft_outer_kernel(l_ref, ar_ref, ai_ref, br_ref, bi_ref):
    s1 = ar_ref.shape[0]
    rhs = jnp.concatenate([ar_ref[...], ai_ref[...]], axis=0)
    r = jnp.dot(l_ref[...], rhs, preferred_element_type=F32)
    br_ref[...] = r[:s1].astype(BF16)
    bi_ref[...] = r[s1:].astype(BF16)


def _dft_outer(a_r, a_i, lmat, tok0, n_seq, s1, *, tn):
    t, d = a_r.shape
    cols = DFT_INNER * d
    view = lambda a: a.reshape(t // DFT_INNER, cols)
    row0 = tok0 // DFT_INNER // s1
    assert row0 * s1 * DFT_INNER == tok0
    blk = pl.BlockSpec((s1, tn), lambda b, j: (row0 + b, j))
    oblk = pl.BlockSpec((s1, tn), lambda b, j: (b, j))
    out = jax.ShapeDtypeStruct((n_seq * s1, cols), BF16)
    pipelined = 4 * _nbytes((s1, tn), BF16) + _nbytes(lmat.shape, BF16)
    resident = _nbytes((2 * s1, tn), BF16) + _nbytes((2 * s1, tn), F32)
    return pl.pallas_call(
        _dft_outer_kernel,
        out_shape=(out, out),
        grid=(n_seq, cols // tn),
        in_specs=[pl.BlockSpec(lmat.shape, lambda b, j: (0, 0)), blk, blk],
        out_specs=(oblk, oblk),
        compiler_params=_params(("parallel", "parallel"), pipelined, resident),
        name="dft_outer",
    )(lmat, view(a_r), view(a_i))


def _dft_inner_kernel(scale, g_ref, br_ref, bi_ref, y_ref):
    kb = g_ref.shape[0]
    d = br_ref.shape[1]
    for u in range(kb):
        rows = slice(u * DFT_INNER, (u + 1) * DFT_INNER)
        rhs = jnp.concatenate([br_ref[rows, :], bi_ref[rows, :]], axis=0)
        r = jnp.dot(g_ref[u], rhs, preferred_element_type=F32) * scale
        y_ref[0, :, u * d:(u + 1) * d] = r.astype(BF16)


def _dft_inner(b_r, b_i, gtab, n_seq, s1, d, scale, *, kb):
    s = s1 * DFT_INNER
    view = lambda a: a.reshape(n_seq * s, d)
    steps = s1 // kb
    blk = pl.BlockSpec((kb * DFT_INNER, d), lambda b, k: (b * steps + k, 0))
    pipelined = 2 * _nbytes((kb * DFT_INNER, d), BF16) + _nbytes((kb, DFT_INNER, 2 * DFT_INNER), BF16) \
        + _nbytes((DFT_INNER, kb * d), BF16)
    resident = _nbytes((2 * DFT_INNER, d), BF16) + 2 * _nbytes((DFT_INNER, d), F32)
    y = pl.pallas_call(
        functools.partial(_dft_inner_kernel, scale),
        out_shape=jax.ShapeDtypeStruct((n_seq, DFT_INNER, s1 * d), BF16),
        grid=(n_seq, steps),
        in_specs=[pl.BlockSpec((kb, DFT_INNER, 2 * DFT_INNER), lambda b, k: (k, 0, 0)), blk, blk],
        out_specs=pl.BlockSpec((1, DFT_INNER, kb * d), lambda b, k: (b, 0, k)),
        compiler_params=_params(("parallel", "parallel"), pipelined, resident),
        name="dft_inner",
    )(gtab, view(b_r), view(b_i))
    return y.reshape(n_seq * s, d)


def _fourier_mix(x, gain, groups, *, tm):
    t, d = x.shape
    gw = d // FOURIER_GROUPS
    a_r, a_i = _channel_dft(x, gain, jnp.asarray(_channel_dft_matrix(gw)).astype(BF16), tm=tm)
    ys = []
    for tok0, n_seq, s in groups:
        s1 = s // DFT_INNER
        lmat = jnp.asarray(_outer_dft_matrix(s1)).astype(BF16)
        b_r, b_i = _dft_outer(a_r, a_i, lmat, tok0, n_seq, s1, tn=4 * d)
        gtab = _inner_dft_tables(s1).astype(BF16)
        ys.append(_dft_inner(b_r, b_i, gtab, n_seq, s1, d, float((s * gw) ** -0.5), kb=4))
    return ys[0] if len(ys) == 1 else jnp.concatenate(ys, axis=0)


def _final_norm_kernel(x_ref, g_ref, o_ref):
    o_ref[...] = _rms(x_ref[...], g_ref[...])


def _final_norm(x, gain, tok0, n_tok, *, tm):
    d = x.shape[1]
    blk0 = tok0 // tm
    return pl.pallas_call(
        _final_norm_kernel,
        out_shape=jax.ShapeDtypeStruct((n_tok, d), F32),
        grid=(n_tok // tm,),
        in_specs=[pl.BlockSpec((tm, d), lambda i: (blk0 + i, 0)), pl.BlockSpec((1, d), lambda i: (0, 0))],
        out_specs=pl.BlockSpec((tm, d), lambda i: (i, 0)),
        compiler_params=_params(("parallel",), 2 * _nbytes((tm, d), F32), _nbytes((tm, d), F32)),
        name="final_norm",
    )(x, gain.reshape(1, d))


def _trunk(xs, seqs, p):
    t, d = xs.shape
    tm = _tile(math.gcd(*[n for _, n in seqs]), 1024)
    depth = p["ffn1_norm"].shape[0]
    a_width = p["ab_rpb"].shape[1] * HEAD_DIM
    b_width = d - a_width
    seq_starts = [s for s, _ in seqs]
    seq_ends = [s + n for s, n in seqs]
    seg_groups = [s // Q_TOK for s in seq_starts] + [seq_ends[-1] // Q_TOK]
    for s, n in seqs:
        assert s % tm == 0 and n % tm == 0 and n >= K_TOK and n % (DFT_INNER * BF16_SUBLANES // 2) == 0
    groups = []
    for s, n in seqs:
        if groups and groups[-1][2] == n and groups[-1][0] + groups[-1][1] * n == s:
            groups[-1] = (groups[-1][0], groups[-1][1] + 1, n)
        else:
            groups.append((s, 1, n))
    bf = lambda a: a.astype(BF16)
    tf = _tile(p["ffn1_w_gate"].shape[2], 512)
    tn = _tile(d, 1024)
    x = xs
    for i in range(depth):
        x = _ffn(x, p["ffn1_norm"][i], bf(p["ffn1_w_gate"][i]), bf(p["ffn1_w_up"][i]), bf(p["ffn1_w_down"][i]),
                 tm=tm, tf=tf)
        j = i // 2
        if i % 2 == 0:
            w_in = bf(p["ab_w_in"][j])
            qkv = _qkv(x, p["mix_norm"][i], w_in, 3 * a_width, tm=tm, tn=_tile(a_width, 1024))
            c = _glu(x, p["mix_norm"][i], w_in, 3 * a_width, 3 * a_width + b_width, b_width, tm=tm, tn=_tile(b_width, 512))
            att = _attention(qkv, _attn_bias_tables(p["ab_rpb"][j]), seg_groups)
            cc = _conv_module(c, p["ab_conv_w"][j], p["ab_conv_b"][j], p["ab_ln_g"][j], p["ab_ln_b"][j],
                              seq_starts, seq_ends, tm=min(tm, 512))
            x = _outproj(x, att, cc, bf(p["ab_w_out"][j]), tm=tm, tn=tn)
        else:
            y = _fourier_mix(x, p["mix_norm"][i], groups, tm=tm)
            x = _resmm(x, y, bf(p["c_w_out"][j]), tm=tm, tn=tn)
        x = _ffn(x, p["ffn2_norm"][i], bf(p["ffn2_w_gate"][i]), bf(p["ffn2_w_up"][i]), bf(p["ffn2_w_down"][i]),
                 tm=tm, tf=tf)
    return x


def kernel(x_prompt, x_sample, ffn1_norm, ffn1_w_gate, ffn1_w_up, ffn1_w_down, mix_norm, ab_w_in, ab_rpb, ab_conv_w, ab_conv_b, ab_ln_g, ab_ln_b, ab_w_out, c_w_out, ffn2_norm, ffn2_w_gate, ffn2_w_up, ffn2_w_down, final_norm):
    params = dict(ffn1_norm=ffn1_norm, ffn1_w_gate=ffn1_w_gate, ffn1_w_up=ffn1_w_up, ffn1_w_down=ffn1_w_down,
                  mix_norm=mix_norm, ab_w_in=ab_w_in, ab_rpb=ab_rpb, ab_conv_w=ab_conv_w, ab_conv_b=ab_conv_b,
                  ab_ln_g=ab_ln_g, ab_ln_b=ab_ln_b, ab_w_out=ab_w_out, c_w_out=c_w_out, ffn2_norm=ffn2_norm,
                  ffn2_w_gate=ffn2_w_gate, ffn2_w_up=ffn2_w_up, ffn2_w_down=ffn2_w_down)
    bp, sp, d = x_prompt.shape
    bs, ss, _ = x_sample.shape
    n_s, n_p = bs * ss, bp * sp
    xs = jnp.concatenate([x_sample.reshape(n_s, d), x_prompt.reshape(n_p, d)], axis=0)
    seqs = [(b * ss, ss) for b in range(bs)] + [(n_s + b * sp, sp) for b in range(bp)]
    x = _trunk(xs, seqs, params)
    tm = _tile(math.gcd(ss, sp), 1024)
    y_sample = _final_norm(x, final_norm, 0, n_s, tm=tm).reshape(bs, ss, d)
    y_prompt = _final_norm(x, final_norm, n_s, n_p, tm=tm).reshape(bp, sp, d)
    return (y_prompt, y_sample)
```

```python
import functools
import math

import numpy as np
import jax
import jax.numpy as jnp
from jax import lax
from jax.experimental import pallas as pl
from jax.experimental.pallas import tpu as pltpu

F32 = jnp.float32
BF16 = jnp.bfloat16

HEAD_DIM = 64
GRID_W = 64
WIN_ROWS = 8
WIN_COLS = 16
CONV_K = 31
FOURIER_GROUPS = 4
NORM_EPS = 1e-6

LANES = 128
F32_SUBLANES = 8
BF16_SUBLANES = 16
VMEM_BUDGET_BYTES = 60000 * 1024

Q_ROWS = 4
K_ROWS = 12
Q_TOK = Q_ROWS * GRID_W
K_TOK = K_ROWS * GRID_W
MASKED = -1e30
DFT_INNER = 128
CONV_HALO = 16


def _vmem_limit(pipelined_bytes, resident_bytes=0):
    return int(min(VMEM_BUDGET_BYTES, 2 * pipelined_bytes + resident_bytes + (4 << 20)))


def _nbytes(shape, dtype):
    return math.prod(shape) * jnp.dtype(dtype).itemsize


def _params(semantics, pipelined_bytes, resident_bytes=0):
    return pltpu.CompilerParams(dimension_semantics=semantics,
                                vmem_limit_bytes=_vmem_limit(pipelined_bytes, resident_bytes))


def _rms(x, gain):
    return x * lax.rsqrt(jnp.mean(x * x, axis=-1, keepdims=True) + NORM_EPS) * gain


def _tile(n, preferred):
    best = LANES
    for c in range(LANES, min(n, preferred) + 1, LANES):
        if n % c == 0:
            best = c
    assert n % best == 0
    return best


def _seg_select(i, bounds, values):
    out = values[-1]
    for s in range(len(values) - 2, -1, -1):
        out = jnp.where(i < bounds[s + 1], values[s], out)
    return out


def _ffn_kernel(out_norm, x_ref, g_ref, wg_ref, wu_ref, wd_ref, og_ref, o_ref, h_ref):
    @pl.when(pl.program_id(1) == 0)
    def _():
        x = x_ref[...]
        h_ref[...] = _rms(x, g_ref[...]).astype(BF16)
        o_ref[...] = x

    h = h_ref[...]
    gate = jnp.dot(h, wg_ref[...], preferred_element_type=F32)
    up = jnp.dot(h, wu_ref[...], preferred_element_type=F32)
    act = (gate * jax.nn.sigmoid(gate)) * (0.5 * up)
    o_ref[...] += jnp.dot(act.astype(BF16), wd_ref[...], preferred_element_type=F32)

    if out_norm:
        @pl.when(pl.program_id(1) == pl.num_programs(1) - 1)
        def _():
            o_ref[...] = _rms(o_ref[...], og_ref[...])


def _ffn(x, gain, w_gate, w_up, w_down, *, tm, tf, out_gain=None, tok0=0, n_tok=None):
    t, d = x.shape
    n_tok = t if n_tok is None else n_tok
    f = w_gate.shape[1]
    blk0 = tok0 // tm
    assert blk0 * tm == tok0 and n_tok % tm == 0
    pipelined = (_nbytes((tm, d), F32) * 2 + 2 * _nbytes((d, tf), BF16) + _nbytes((tf, d), BF16))
    resident = _nbytes((tm, d), BF16) + 3 * _nbytes((tm, tf), F32)
    og = gain if out_gain is None else out_gain
    return pl.pallas_call(
        functools.partial(_ffn_kernel, out_gain is not None),
        out_shape=jax.ShapeDtypeStruct((n_tok, d), F32),
        grid=(n_tok // tm, f // tf),
        in_specs=[
            pl.BlockSpec((tm, d), lambda i, j: (blk0 + i, 0)),
            pl.BlockSpec((1, d), lambda i, j: (0, 0)),
            pl.BlockSpec((d, tf), lambda i, j: (0, j)),
            pl.BlockSpec((d, tf), lambda i, j: (0, j)),
            pl.BlockSpec((tf, d), lambda i, j: (j, 0)),
            pl.BlockSpec((1, d), lambda i, j: (0, 0)),
        ],
        out_specs=pl.BlockSpec((tm, d), lambda i, j: (i, 0)),
        scratch_shapes=[pltpu.VMEM((tm, d), BF16)],
        compiler_params=_params(("parallel", "arbitrary"), pipelined, resident),
        name="ffn",
    )(x, gain.reshape(1, d), w_gate, w_up, w_down, og.reshape(1, d))


def _qkv_kernel(x_ref, g_ref, w_ref, o_ref, h_ref):
    @pl.when(pl.program_id(1) == 0)
    def _():
        h_ref[...] = _rms(x_ref[...], g_ref[...]).astype(BF16)

    r = jnp.dot(h_ref[...], w_ref[...], preferred_element_type=F32)
    for p in range(o_ref.shape[0]):
        o_ref[p] = r[:, p * LANES:(p + 1) * LANES].astype(BF16)


def _qkv(x, gain, w_in, n_cols, *, tm, tn):
    t, d = x.shape
    pipelined = _nbytes((tm, d), F32) + _nbytes((d, tn), BF16) + _nbytes((tm, tn), BF16)
    resident = _nbytes((tm, d), BF16) + _nbytes((tm, tn), F32)
    return pl.pallas_call(
        _qkv_kernel,
        out_shape=jax.ShapeDtypeStruct((n_cols // LANES, t, LANES), BF16),
        grid=(t // tm, n_cols // tn),
        in_specs=[
            pl.BlockSpec((tm, d), lambda i, j: (i, 0)),
            pl.BlockSpec((1, d), lambda i, j: (0, 0)),
            pl.BlockSpec((d, tn), lambda i, j: (0, j)),
        ],
        out_specs=pl.BlockSpec((tn // LANES, tm, LANES), lambda i, j: (j, i, 0)),
        scratch_shapes=[pltpu.VMEM((tm, d), BF16)],
        compiler_params=_params(("parallel", "arbitrary"), pipelined, resident),
        name="qkv_proj",
    )(x, gain.reshape(1, d), w_in)


def _glu_kernel(x_ref, g_ref, wa_ref, wg_ref, o_ref, h_ref):
    @pl.when(pl.program_id(1) == 0)
    def _():
        h_ref[...] = _rms(x_ref[...], g_ref[...]).astype(BF16)

    h = h_ref[...]
    a = jnp.dot(h, wa_ref[...], preferred_element_type=F32)
    g = jnp.dot(h, wg_ref[...], preferred_element_type=F32)
    o_ref[...] = a * jax.nn.sigmoid(g)


def _glu(x, gain, w_in, a_col0, g_col0, width, *, tm, tn):
    t, d = x.shape
    pipelined = _nbytes((tm, d), F32) + 2 * _nbytes((d, tn), BF16) + _nbytes((tm, tn), F32)
    resident = _nbytes((tm, d), BF16) + 2 * _nbytes((tm, tn), F32)
    return pl.pallas_call(
        _glu_kernel,
        out_shape=jax.ShapeDtypeStruct((t, width), F32),
        grid=(t // tm, width // tn),
        in_specs=[
            pl.BlockSpec((tm, d), lambda i, j: (i, 0)),
            pl.BlockSpec((1, d), lambda i, j: (0, 0)),
            pl.BlockSpec((d, tn), lambda i, j: (0, a_col0 // tn + j)),
            pl.BlockSpec((d, tn), lambda i, j: (0, g_col0 // tn + j)),
        ],
        out_specs=pl.BlockSpec((tm, tn), lambda i, j: (i, j)),
        scratch_shapes=[pltpu.VMEM((tm, d), BF16)],
        compiler_params=_params(("parallel", "arbitrary"), pipelined, resident),
        name="glu_proj",
    )(x, gain.reshape(1, d), w_in, w_in)


N_ROW_OFFSETS = 2 * WIN_ROWS - 1
MASKED_ROW = N_ROW_OFFSETS


def _attn_bias_rows(rpb):
    n_heads = rpb.shape[0]
    pad = GRID_W - WIN_COLS
    padded = jnp.pad(rpb.astype(F32), ((0, 0), (0, 0), (pad, pad)))
    rows = jnp.stack([padded[:, :, GRID_W - 1 - c:2 * GRID_W - 1 - c] for c in range(GRID_W)], axis=2)
    c = np.arange(GRID_W)[:, None]
    kc = np.arange(GRID_W)[None, :]
    c0 = np.clip(c - WIN_COLS // 2, 0, GRID_W - WIN_COLS)
    c_ok = (kc >= c0) & (kc < c0 + WIN_COLS)
    rows = jnp.where(c_ok[None, None], rows, MASKED)
    rows = jnp.concatenate([rows, jnp.full((n_heads, 1, GRID_W, GRID_W), MASKED, F32)], axis=1)
    return jnp.concatenate([rows, rows], axis=3)


def _attn_kernel(lo_of, hi_of, q_ref, k0_ref, k1_ref, k2_ref, v0_ref, v1_ref, v2_ref, t_ref, o_ref, b_ref):
    i = pl.program_id(1)
    lo = lo_of(i)
    hi = hi_of(i)
    n_pairs = q_ref.shape[0]
    lane = lax.broadcasted_iota(jnp.int32, (GRID_W, LANES), 1)

    @pl.when((i == lo) | (i == lo + 1) | (i == hi - 1))
    def _():
        case = jnp.where(i == lo, 0, jnp.where(i == hi - 1, 2, 1))
        for qr in range(Q_ROWS):
            first = jnp.where(case == 0, 0, jnp.where(case == 1, qr, Q_ROWS))

            def entry(j):
                ok = (j >= first) & (j < first + WIN_ROWS)
                return jnp.where(ok, j - case * Q_ROWS - qr + (WIN_ROWS - 1), MASKED_ROW)

            for h in range(2 * n_pairs):
                r0 = (h % 2) * Q_TOK + qr * GRID_W
                for j in range(0, K_ROWS, 2):
                    left = t_ref[h, entry(j)]
                    right = t_ref[h, entry(j + 1)]
                    b_ref[h // 2, r0:r0 + GRID_W, j * GRID_W:(j + 2) * GRID_W] = jnp.where(lane < GRID_W, left, right)

    qlane = lax.broadcasted_iota(jnp.int32, (Q_TOK, LANES), 1)
    for p in range(n_pairs):
        q = q_ref[p]
        zero = jnp.zeros_like(q)
        q2 = jnp.concatenate([jnp.where(qlane < HEAD_DIM, q, zero), jnp.where(qlane >= HEAD_DIM, q, zero)], axis=0)
        k = jnp.concatenate([k0_ref[p], k1_ref[p], k2_ref[p]], axis=0)
        v = jnp.concatenate([v0_ref[p], v1_ref[p], v2_ref[p]], axis=0)
        s = lax.dot_general(q2, k, (((1,), (1,)), ((), ())), preferred_element_type=F32)
        s = s * (HEAD_DIM ** -0.5) + b_ref[p]
        e = jnp.exp(s - jnp.max(s, axis=-1, keepdims=True))
        denom = jnp.sum(e, axis=-1, keepdims=True)
        o2 = jnp.dot(e.astype(BF16), v, preferred_element_type=F32) / denom
        o_ref[p] = jnp.where(qlane < HEAD_DIM, o2[:Q_TOK], o2[Q_TOK:]).astype(BF16)


def _attention(qkv, bias_rows, seg_groups, *, pairs_per_step):
    n_pairs = qkv.shape[0] // 3
    t = qkv.shape[1]
    pb = pairs_per_step
    assert n_pairs % pb == 0
    lo_of = lambda i: _seg_select(i, seg_groups, seg_groups[:-1])
    hi_of = lambda i: _seg_select(i, seg_groups, seg_groups[1:])
    window = lambda i: jnp.clip(i - 1, lo_of(i), hi_of(i) - K_ROWS // Q_ROWS)

    blk = (pb, Q_TOK, LANES)
    kv_specs = [pl.BlockSpec(blk, functools.partial(lambda p, i, off, j: (off + p, window(i) + j, 0), off=off, j=j))
                for off in (n_pairs // pb, 2 * n_pairs // pb) for j in range(K_ROWS // Q_ROWS)]
    tab_blk = (2 * pb,) + bias_rows.shape[1:]
    pipelined = 8 * _nbytes(blk, BF16) + _nbytes(tab_blk, F32)
    resident = _nbytes((pb, 2 * Q_TOK, K_TOK), F32) + 3 * pb * _nbytes((2 * Q_TOK, K_TOK), F32)
    return pl.pallas_call(
        functools.partial(_attn_kernel, lo_of, hi_of),
        out_shape=jax.ShapeDtypeStruct((n_pairs, t, LANES), BF16),
        grid=(n_pairs // pb, t // Q_TOK),
        in_specs=[pl.BlockSpec(blk, lambda p, i: (p, i, 0))] + kv_specs + [
            pl.BlockSpec(tab_blk, lambda p, i: (p, 0, 0, 0))],
        out_specs=pl.BlockSpec(blk, lambda p, i: (p, i, 0)),
        scratch_shapes=[pltpu.VMEM((pb, 2 * Q_TOK, K_TOK), F32)],
        compiler_params=_params(("arbitrary", "arbitrary"), pipelined, resident),
        name="nbr_attention",
    )(qkv, *([qkv] * 6), bias_rows)


def _conv_kernel(first_tiles, last_tiles, prev_ref, cur_ref, next_ref, w_ref, cb_ref, lg_ref, lb_ref,
                 o_ref, buf_ref, sh_ref, y_ref):
    i = pl.program_id(0)
    tm, c = cur_ref.shape
    is_first = functools.reduce(jnp.logical_or, [i == s for s in first_tiles])
    is_last = functools.reduce(jnp.logical_or, [i == s for s in last_tiles])
    buf_ref[0:CONV_HALO, :] = jnp.where(is_first, 0.0, prev_ref[...])
    buf_ref[CONV_HALO:CONV_HALO + tm, :] = cur_ref[...]
    buf_ref[CONV_HALO + tm:, :] = jnp.where(is_last, 0.0, next_ref[...])
    chunk = 64
    base = CONV_HALO - CONV_K // 2
    n_sh = sh_ref.shape[1]
    for l0 in range(0, c, LANES):
        lanes = slice(l0, l0 + LANES)
        for o in range(1, F32_SUBLANES):
            sh_ref[o] = buf_ref[o:o + n_sh, lanes]
        def chunk_body(ci, carry, lanes=lanes):
            t0 = pl.multiple_of(ci * chunk, chunk)
            acc = jnp.broadcast_to(cb_ref[:, lanes], (chunk, LANES))
            for k in range(CONV_K):
                o = (base + k) % F32_SUBLANES
                rows = pl.ds(t0 + (base + k - o), chunk)
                src = buf_ref[rows, lanes] if o == 0 else sh_ref[o, rows, :]
                acc = acc + w_ref[k:k + 1, lanes] * src
            y_ref[pl.ds(t0, chunk), lanes] = acc
            return carry

        lax.fori_loop(0, tm // chunk, chunk_body, 0)
    y = y_ref[...]
    mu = jnp.mean(y, axis=-1, keepdims=True)
    yc = y - mu
    var = jnp.mean(yc * yc, axis=-1, keepdims=True)
    z = yc * lax.rsqrt(var + NORM_EPS) * lg_ref[...] + lb_ref[...]
    o_ref[...] = (z * jax.nn.sigmoid(z)).astype(BF16)


def _conv_module(c, conv_w, conv_b, ln_g, ln_b, seq_starts, seq_ends, *, tm):
    t, width = c.shape
    first_tiles = tuple(s // tm for s in seq_starts)
    last_tiles = tuple(e // tm - 1 for e in seq_ends)
    hb = tm // CONV_HALO
    n_hb = t // CONV_HALO
    pipelined = _nbytes((tm + 2 * CONV_HALO, width), F32) + _nbytes((tm, width), BF16)
    resident = _nbytes((2 * tm + 2 * CONV_HALO, width), F32) + 3 * _nbytes((tm, width), F32)
    row = lambda a: a.reshape(1, width)
    return pl.pallas_call(
        functools.partial(_conv_kernel, first_tiles, last_tiles),
        out_shape=jax.ShapeDtypeStruct((t, width), BF16),
        grid=(t // tm,),
        in_specs=[
            pl.BlockSpec((CONV_HALO, width), lambda i: (jnp.maximum(i * hb - 1, 0), 0)),
            pl.BlockSpec((tm, width), lambda i: (i, 0)),
            pl.BlockSpec((CONV_HALO, width), lambda i: (jnp.minimum((i + 1) * hb, n_hb - 1), 0)),
            pl.BlockSpec((CONV_K, width), lambda i: (0, 0)),
            pl.BlockSpec((1, width), lambda i: (0, 0)),
            pl.BlockSpec((1, width), lambda i: (0, 0)),
            pl.BlockSpec((1, width), lambda i: (0, 0)),
        ],
        out_specs=pl.BlockSpec((tm, width), lambda i: (i, 0)),
        scratch_shapes=[pltpu.VMEM((tm + 2 * CONV_HALO, width), F32),
                        pltpu.VMEM((F32_SUBLANES, tm + 2 * CONV_HALO - F32_SUBLANES, LANES), F32),
                        pltpu.VMEM((tm, width), F32)],
        compiler_params=_params(("parallel",), pipelined, resident),
        name="conv_module",
    )(c, c, c, conv_w, row(conv_b), row(ln_g), row(ln_b))


def _outproj_kernel(x_ref, att_ref, cc_ref, w1_ref, w2_ref, o_ref):
    att = jnp.concatenate([att_ref[p] for p in range(att_ref.shape[0])], axis=1)
    acc = jnp.dot(att, w1_ref[...], preferred_element_type=F32)
    acc = acc + jnp.dot(cc_ref[...], w2_ref[...], preferred_element_type=F32)
    o_ref[...] = x_ref[...] + acc


def _outproj(x, att, cc, w_out, *, tm, tn):
    t, d = x.shape
    n_pairs = att.shape[0]
    wa = n_pairs * LANES
    wb = cc.shape[1]
    assert wa == wb and wa + wb == w_out.shape[0]
    pipelined = (2 * _nbytes((tm, tn), F32) + _nbytes((tm, wa), BF16) + _nbytes((tm, wb), BF16)
                 + _nbytes((wa + wb, tn), BF16))
    resident = 2 * _nbytes((tm, tn), F32) + _nbytes((tm, wa), BF16)
    return pl.pallas_call(
        _outproj_kernel,
        out_shape=jax.ShapeDtypeStruct((t, d), F32),
        grid=(t // tm, d // tn),
        in_specs=[
            pl.BlockSpec((tm, tn), lambda i, j: (i, j)),
            pl.BlockSpec((n_pairs, tm, LANES), lambda i, j: (0, i, 0)),
            pl.BlockSpec((tm, wb), lambda i, j: (i, 0)),
            pl.BlockSpec((wa, tn), lambda i, j: (0, j)),
            pl.BlockSpec((wb, tn), lambda i, j: (1, j)),
        ],
        out_specs=pl.BlockSpec((tm, tn), lambda i, j: (i, j)),
        compiler_params=_params(("parallel", "arbitrary"), pipelined, resident),
        name="attn_conv_out_proj",
    )(x, att, cc, w_out, w_out)


def _resmm_kernel(x_ref, a_ref, w_ref, o_ref):
    o_ref[...] = x_ref[...] + jnp.dot(a_ref[...], w_ref[...], preferred_element_type=F32)


def _resmm(x, a, w, *, tm, tn):
    t, d = x.shape
    kdim = a.shape[1]
    pipelined = 2 * _nbytes((tm, tn), F32) + _nbytes((tm, kdim), BF16) + _nbytes((kdim, tn), BF16)
    resident = 2 * _nbytes((tm, tn), F32)
    return pl.pallas_call(
        _resmm_kernel,
        out_shape=jax.ShapeDtypeStruct((t, d), F32),
        grid=(t // tm, d // tn),
        in_specs=[
            pl.BlockSpec((tm, tn), lambda i, j: (i, j)),
            pl.BlockSpec((tm, kdim), lambda i, j: (i, 0)),
            pl.BlockSpec((kdim, tn), lambda i, j: (0, j)),
        ],
        out_specs=pl.BlockSpec((tm, tn), lambda i, j: (i, j)),
        compiler_params=_params(("parallel", "arbitrary"), pipelined, resident),
        name="fourier_out_proj",
    )(x, a, w)


def _cos_sin(num, den):
    ang = (2.0 * np.pi / den) * (num % den).astype(np.float64)
    return np.cos(ang).astype(np.float32), np.sin(ang).astype(np.float32)


def _channel_dft_matrix(n):
    idx = np.arange(n)
    c, s = _cos_sin(idx[:, None] * idx[None, :], n)
    return np.concatenate([c, -s], axis=1)


def _outer_dft_matrix(s1):
    idx = np.arange(s1)
    c, s = _cos_sin(idx[:, None] * idx[None, :], s1)
    return np.block([[c, s], [-s, c]])


def _inner_dft_tables(s1):
    s = s1 * DFT_INNER
    k1 = jnp.arange(s1, dtype=jnp.int32)[:, None, None]
    k2 = jnp.arange(DFT_INNER, dtype=jnp.int32)[None, :, None]
    n2 = jnp.arange(DFT_INNER, dtype=jnp.int32)[None, None, :]
    num = (n2 * (k1 + s1 * k2)) % s
    ang = num.astype(F32) * F32(2.0 * np.pi / s)
    return jnp.concatenate([jnp.cos(ang), jnp.sin(ang)], axis=2)


def _chdft_kernel(x_ref, g_ref, d_ref, ar_ref, ai_ref):
    h = _rms(x_ref[...], g_ref[...]).astype(BF16)
    gw = d_ref.shape[0]
    for q in range(h.shape[1] // gw):
        r = jnp.dot(h[:, q * gw:(q + 1) * gw], d_ref[...], preferred_element_type=F32)
        ar_ref[:, q * gw:(q + 1) * gw] = r[:, :gw].astype(BF16)
        ai_ref[:, q * gw:(q + 1) * gw] = r[:, gw:].astype(BF16)


def _channel_dft(x, gain, dmat, *, tm):
    t, d = x.shape
    gw = dmat.shape[0]
    pipelined = _nbytes((tm, d), F32) + 2 * _nbytes((tm, d), BF16) + _nbytes(dmat.shape, BF16)
    resident = _nbytes((tm, d), BF16) + 2 * _nbytes((tm, 2 * gw), F32) + _nbytes((tm, d), F32)
    out = jax.ShapeDtypeStruct((t, d), BF16)
    return pl.pallas_call(
        _chdft_kernel,
        out_shape=(out, out),
        grid=(t // tm,),
        in_specs=[
            pl.BlockSpec((tm, d), lambda i: (i, 0)),
            pl.BlockSpec((1, d), lambda i: (0, 0)),
            pl.BlockSpec(dmat.shape, lambda i: (0, 0)),
        ],
        out_specs=(pl.BlockSpec((tm, d), lambda i: (i, 0)), pl.BlockSpec((tm, d), lambda i: (i, 0))),
        compiler_params=_params(("parallel",), pipelined, resident),
        name="channel_dft",
    )(x, gain.reshape(1, d), dmat)


def _dft_outer_kernel(l_ref, ar_ref, ai_ref, br_ref, bi_ref):
    s1 = ar_ref.shape[0]
    rhs = jnp.concatenate([ar_ref[...], ai_ref[...]], axis=0)
    r = jnp.dot(l_ref[...], rhs, preferred_element_type=F32)
    br_ref[...] = r[:s1].astype(BF16)
    bi_ref[...] = r[s1:].astype(BF16)


def _dft_outer(a_r, a_i, lmat, tok0, n_seq, s1, *, tn):
    t, d = a_r.shape
    cols = DFT_INNER * d
    view = lambda a: a.reshape(t // DFT_INNER, cols)
    row0 = tok0 // DFT_INNER // s1
    assert row0 * s1 * DFT_INNER == tok0
    blk = pl.BlockSpec((s1, tn), lambda b, j: (row0 + b, j))
    oblk = pl.BlockSpec((s1, tn), lambda b, j: (b, j))
    out = jax.ShapeDtypeStruct((n_seq * s1, cols), BF16)
    pipelined = 4 * _nbytes((s1, tn), BF16) + _nbytes(lmat.shape, BF16)
    resident = _nbytes((2 * s1, tn), BF16) + _nbytes((2 * s1, tn), F32)
    return pl.pallas_call(
        _dft_outer_kernel,
        out_shape=(out, out),
        grid=(n_seq, cols // tn),
        in_specs=[pl.BlockSpec(lmat.shape, lambda b, j: (0, 0)), blk, blk],
        out_specs=(oblk, oblk),
        compiler_params=_params(("parallel", "parallel"), pipelined, resident),
        name="dft_outer",
    )(lmat, view(a_r), view(a_i))


def _dft_inner_kernel(scale, g_ref, br_ref, bi_ref, y_ref):
    kb = g_ref.shape[0]
    d = br_ref.shape[1]
    for u in range(kb):
        rows = slice(u * DFT_INNER, (u + 1) * DFT_INNER)
        rhs = jnp.concatenate([br_ref[rows, :], bi_ref[rows, :]], axis=0)
        r = jnp.dot(g_ref[u], rhs, preferred_element_type=F32) * scale
        y_ref[0, :, u * d:(u + 1) * d] = r.astype(BF16)


def _dft_inner(b_r, b_i, gtab, n_seq, s1, d, scale, *, kb):
    s = s1 * DFT_INNER
    view = lambda a: a.reshape(n_seq * s, d)
    steps = s1 // kb
    blk = pl.BlockSpec((kb * DFT_INNER, d), lambda b, k: (b * steps + k, 0))
    pipelined = 2 * _nbytes((kb * DFT_INNER, d), BF16) + _nbytes((kb, DFT_INNER, 2 * DFT_INNER), BF16) \
        + _nbytes((DFT_INNER, kb * d), BF16)
    resident = _nbytes((2 * DFT_INNER, d), BF16) + 2 * _nbytes((DFT_INNER, d), F32)
    y = pl.pallas_call(
        functools.partial(_dft_inner_kernel, scale),
        out_shape=jax.ShapeDtypeStruct((n_seq, DFT_INNER, s1 * d), BF16),
        grid=(n_seq, steps),
        in_specs=[pl.BlockSpec((kb, DFT_INNER, 2 * DFT_INNER), lambda b, k: (k, 0, 0)), blk, blk],
        out_specs=pl.BlockSpec((1, DFT_INNER, kb * d), lambda b, k: (b, 0, k)),
        compiler_params=_params(("parallel", "parallel"), pipelined, resident),
        name="dft_inner",
    )(gtab, view(b_r), view(b_i))
    return y.reshape(n_seq * s, d)


def _fourier_mix(x, gain, groups, *, tm):
    t, d = x.shape
    gw = d // FOURIER_GROUPS
    a_r, a_i = _channel_dft(x, gain, jnp.asarray(_channel_dft_matrix(gw)).astype(BF16), tm=tm)
    ys = []
    for tok0, n_seq, s in groups:
        s1 = s // DFT_INNER
        lmat = jnp.asarray(_outer_dft_matrix(s1)).astype(BF16)
        b_r, b_i = _dft_outer(a_r, a_i, lmat, tok0, n_seq, s1, tn=4 * d)
        gtab = _inner_dft_tables(s1).astype(BF16)
        ys.append(_dft_inner(b_r, b_i, gtab, n_seq, s1, d, float((s * gw) ** -0.5), kb=4))
    return ys[0] if len(ys) == 1 else jnp.concatenate(ys, axis=0)


def _trunk(xs, seqs, out_ranges, p):
    t, d = xs.shape
    tm = _tile(math.gcd(*[n for _, n in seqs]), 1024)
    depth = p["ffn1_norm"].shape[0]
    a_width = p["ab_rpb"].shape[1] * HEAD_DIM
    b_width = d - a_width
    seq_starts = [s for s, _ in seqs]
    seq_ends = [s + n for s, n in seqs]
    seg_groups = [s // Q_TOK for s in seq_starts] + [seq_ends[-1] // Q_TOK]
    for s, n in seqs:
        assert s % tm == 0 and n % tm == 0 and n >= K_TOK and n % (DFT_INNER * BF16_SUBLANES // 2) == 0
    groups = []
    for s, n in seqs:
        if groups and groups[-1][2] == n and groups[-1][0] + groups[-1][1] * n == s:
            groups[-1] = (groups[-1][0], groups[-1][1] + 1, n)
        else:
            groups.append((s, 1, n))
    bf = lambda a: a.astype(BF16)
    tf = _tile(p["ffn1_w_gate"].shape[2], 512)
    tn = _tile(d, 1024)
    x = xs
    for i in range(depth):
        x = _ffn(x, p["ffn1_norm"][i], bf(p["ffn1_w_gate"][i]), bf(p["ffn1_w_up"][i]), bf(p["ffn1_w_down"][i]),
                 tm=tm, tf=tf)
        j = i // 2
        if i % 2 == 0:
            w_in = bf(p["ab_w_in"][j])
            qkv = _qkv(x, p["mix_norm"][i], w_in, 3 * a_width, tm=tm, tn=_tile(a_width, 1024))
            c = _glu(x, p["mix_norm"][i], w_in, 3 * a_width, 3 * a_width + b_width, b_width, tm=tm, tn=_tile(b_width, 512))
            att = _attention(qkv, _attn_bias_rows(p["ab_rpb"][j]), seg_groups,
                             pairs_per_step=math.gcd(a_width // LANES, 4))
            cc = _conv_module(c, p["ab_conv_w"][j], p["ab_conv_b"][j], p["ab_ln_g"][j], p["ab_ln_b"][j],
                              seq_starts, seq_ends, tm=min(tm, 512))
            x = _outproj(x, att, cc, bf(p["ab_w_out"][j]), tm=tm, tn=tn)
        else:
            y = _fourier_mix(x, p["mix_norm"][i], groups, tm=tm)
            x = _resmm(x, y, bf(p["c_w_out"][j]), tm=tm, tn=tn)
        ffn2 = functools.partial(_ffn, x, p["ffn2_norm"][i], bf(p["ffn2_w_gate"][i]), bf(p["ffn2_w_up"][i]),
                                 bf(p["ffn2_w_down"][i]), tm=tm, tf=tf)
        if i < depth - 1:
            x = ffn2()
    return [ffn2(out_gain=p["final_norm"], tok0=tok0, n_tok=n_tok) for tok0, n_tok in out_ranges]


def kernel(x_prompt, x_sample, ffn1_norm, ffn1_w_gate, ffn1_w_up, ffn1_w_down, mix_norm, ab_w_in, ab_rpb, ab_conv_w, ab_conv_b, ab_ln_g, ab_ln_b, ab_w_out, c_w_out, ffn2_norm, ffn2_w_gate, ffn2_w_up, ffn2_w_down, final_norm):
    params = dict(ffn1_norm=ffn1_norm, ffn1_w_gate=ffn1_w_gate, ffn1_w_up=ffn1_w_up, ffn1_w_down=ffn1_w_down,
                  mix_norm=mix_norm, ab_w_in=ab_w_in, ab_rpb=ab_rpb, ab_conv_w=ab_conv_w, ab_conv_b=ab_conv_b,
                  ab_ln_g=ab_ln_g, ab_ln_b=ab_ln_b, ab_w_out=ab_w_out, c_w_out=c_w_out, ffn2_norm=ffn2_norm,
                  ffn2_w_gate=ffn2_w_gate, ffn2_w_up=ffn2_w_up, ffn2_w_down=ffn2_w_down, final_norm=final_norm)
    bp, sp, d = x_prompt.shape
    bs, ss, _ = x_sample.shape
    n_s, n_p = bs * ss, bp * sp
    xs = jnp.concatenate([x_sample.reshape(n_s, d), x_prompt.reshape(n_p, d)], axis=0)
    seqs = [(b * ss, ss) for b in range(bs)] + [(n_s + b * sp, sp) for b in range(bp)]
    y_sample, y_prompt = _trunk(xs, seqs, [(0, n_s), (n_s, n_p)], params)
    return (y_prompt.reshape(bp, sp, d), y_sample.reshape(bs, ss, d))
```

```python
import functools
import math

import numpy as np
import jax
import jax.numpy as jnp
from jax import lax
from jax.experimental import pallas as pl
from jax.experimental.pallas import tpu as pltpu

F32 = jnp.float32
BF16 = jnp.bfloat16

HEAD_DIM = 64
GRID_W = 64
WIN_ROWS = 8
WIN_COLS = 16
CONV_K = 31
FOURIER_GROUPS = 4
NORM_EPS = 1e-6

LANES = 128
F32_SUBLANES = 8
BF16_SUBLANES = 16
VMEM_BUDGET_BYTES = 60000 * 1024

Q_ROWS = 4
K_ROWS = 12
Q_TOK = Q_ROWS * GRID_W
K_TOK = K_ROWS * GRID_W
MASKED = -1e30
DFT_RADIX = 16
CONV_HALO = 16


def _vmem_limit(pipelined_bytes, resident_bytes=0):
    return int(min(VMEM_BUDGET_BYTES, 2 * pipelined_bytes + resident_bytes + (4 << 20)))


def _nbytes(shape, dtype):
    return math.prod(shape) * jnp.dtype(dtype).itemsize


def _params(semantics, pipelined_bytes, resident_bytes=0):
    return pltpu.CompilerParams(dimension_semantics=semantics,
                                vmem_limit_bytes=_vmem_limit(pipelined_bytes, resident_bytes))


def _rms(x, gain):
    return x * lax.rsqrt(jnp.mean(x * x, axis=-1, keepdims=True) + NORM_EPS) * gain


def _tile(n, preferred):
    best = LANES
    for c in range(LANES, min(n, preferred) + 1, LANES):
        if n % c == 0:
            best = c
    assert n % best == 0
    return best


def _seg_select(i, bounds, values):
    out = values[-1]
    for s in range(len(values) - 2, -1, -1):
        out = jnp.where(i < bounds[s + 1], values[s], out)
    return out


def _ffn_kernel(out_norm, x_ref, g_ref, wg_ref, wu_ref, wd_ref, og_ref, o_ref, h_ref):
    @pl.when(pl.program_id(1) == 0)
    def _():
        x = x_ref[...]
        h_ref[...] = _rms(x, g_ref[...]).astype(BF16)
        o_ref[...] = x

    h = h_ref[...]
    gate = jnp.dot(h, wg_ref[...], preferred_element_type=F32)
    up = jnp.dot(h, wu_ref[...], preferred_element_type=F32)
    act = (gate * jax.nn.sigmoid(gate)) * (0.5 * up)
    o_ref[...] += jnp.dot(act.astype(BF16), wd_ref[...], preferred_element_type=F32)

    if out_norm:
        @pl.when(pl.program_id(1) == pl.num_programs(1) - 1)
        def _():
            o_ref[...] = _rms(o_ref[...], og_ref[...])


def _ffn(x, gain, w_gate, w_up, w_down, *, tm, tf, out_gain=None, tok0=0, n_tok=None):
    t, d = x.shape
    n_tok = t if n_tok is None else n_tok
    f = w_gate.shape[1]
    blk0 = tok0 // tm
    assert blk0 * tm == tok0 and n_tok % tm == 0
    pipelined = (_nbytes((tm, d), F32) * 2 + 2 * _nbytes((d, tf), BF16) + _nbytes((tf, d), BF16))
    resident = _nbytes((tm, d), BF16) + 3 * _nbytes((tm, tf), F32)
    og = gain if out_gain is None else out_gain
    return pl.pallas_call(
        functools.partial(_ffn_kernel, out_gain is not None),
        out_shape=jax.ShapeDtypeStruct((n_tok, d), F32),
        grid=(n_tok // tm, f // tf),
        in_specs=[
            pl.BlockSpec((tm, d), lambda i, j: (blk0 + i, 0)),
            pl.BlockSpec((1, d), lambda i, j: (0, 0)),
            pl.BlockSpec((d, tf), lambda i, j: (0, j)),
            pl.BlockSpec((d, tf), lambda i, j: (0, j)),
            pl.BlockSpec((tf, d), lambda i, j: (j, 0)),
            pl.BlockSpec((1, d), lambda i, j: (0, 0)),
        ],
        out_specs=pl.BlockSpec((tm, d), lambda i, j: (i, 0)),
        scratch_shapes=[pltpu.VMEM((tm, d), BF16)],
        compiler_params=_params(("parallel", "arbitrary"), pipelined, resident),
        name="ffn",
    )(x, gain.reshape(1, d), w_gate, w_up, w_down, og.reshape(1, d))


def _qkv_kernel(x_ref, g_ref, w_ref, o_ref, h_ref):
    @pl.when(pl.program_id(1) == 0)
    def _():
        h_ref[...] = _rms(x_ref[...], g_ref[...]).astype(BF16)

    r = jnp.dot(h_ref[...], w_ref[...], preferred_element_type=F32)
    for p in range(o_ref.shape[0]):
        o_ref[p] = r[:, p * LANES:(p + 1) * LANES].astype(BF16)


def _qkv(x, gain, w_in, n_cols, *, tm, tn):
    t, d = x.shape
    pipelined = _nbytes((tm, d), F32) + _nbytes((d, tn), BF16) + _nbytes((tm, tn), BF16)
    resident = _nbytes((tm, d), BF16) + _nbytes((tm, tn), F32)
    return pl.pallas_call(
        _qkv_kernel,
        out_shape=jax.ShapeDtypeStruct((n_cols // LANES, t, LANES), BF16),
        grid=(t // tm, n_cols // tn),
        in_specs=[
            pl.BlockSpec((tm, d), lambda i, j: (i, 0)),
            pl.BlockSpec((1, d), lambda i, j: (0, 0)),
            pl.BlockSpec((d, tn), lambda i, j: (0, j)),
        ],
        out_specs=pl.BlockSpec((tn // LANES, tm, LANES), lambda i, j: (j, i, 0)),
        scratch_shapes=[pltpu.VMEM((tm, d), BF16)],
        compiler_params=_params(("parallel", "arbitrary"), pipelined, resident),
        name="qkv_proj",
    )(x, gain.reshape(1, d), w_in)


def _glu_kernel(x_ref, g_ref, wa_ref, wg_ref, o_ref, h_ref):
    @pl.when(pl.program_id(1) == 0)
    def _():
        h_ref[...] = _rms(x_ref[...], g_ref[...]).astype(BF16)

    h = h_ref[...]
    a = jnp.dot(h, wa_ref[...], preferred_element_type=F32)
    g = jnp.dot(h, wg_ref[...], preferred_element_type=F32)
    o_ref[...] = a * jax.nn.sigmoid(g)


def _glu(x, gain, w_in, a_col0, g_col0, width, *, tm, tn):
    t, d = x.shape
    pipelined = _nbytes((tm, d), F32) + 2 * _nbytes((d, tn), BF16) + _nbytes((tm, tn), F32)
    resident = _nbytes((tm, d), BF16) + 2 * _nbytes((tm, tn), F32)
    return pl.pallas_call(
        _glu_kernel,
        out_shape=jax.ShapeDtypeStruct((t, width), F32),
        grid=(t // tm, width // tn),
        in_specs=[
            pl.BlockSpec((tm, d), lambda i, j: (i, 0)),
            pl.BlockSpec((1, d), lambda i, j: (0, 0)),
            pl.BlockSpec((d, tn), lambda i, j: (0, a_col0 // tn + j)),
            pl.BlockSpec((d, tn), lambda i, j: (0, g_col0 // tn + j)),
        ],
        out_specs=pl.BlockSpec((tm, tn), lambda i, j: (i, j)),
        scratch_shapes=[pltpu.VMEM((tm, d), BF16)],
        compiler_params=_params(("parallel", "arbitrary"), pipelined, resident),
        name="glu_proj",
    )(x, gain.reshape(1, d), w_in, w_in)


N_ROW_OFFSETS = 2 * WIN_ROWS - 1
MASKED_ROW = N_ROW_OFFSETS


def _attn_bias_rows(rpb):
    n_heads = rpb.shape[0]
    pad = GRID_W - WIN_COLS
    padded = jnp.pad(rpb.astype(F32), ((0, 0), (0, 0), (pad, pad)))
    rows = jnp.stack([padded[:, :, GRID_W - 1 - c:2 * GRID_W - 1 - c] for c in range(GRID_W)], axis=2)
    c = np.arange(GRID_W)[:, None]
    kc = np.arange(GRID_W)[None, :]
    c0 = np.clip(c - WIN_COLS // 2, 0, GRID_W - WIN_COLS)
    c_ok = (kc >= c0) & (kc < c0 + WIN_COLS)
    rows = jnp.where(c_ok[None, None], rows, MASKED)
    rows = jnp.concatenate([rows, jnp.full((n_heads, 1, GRID_W, GRID_W), MASKED, F32)], axis=1)
    return jnp.concatenate([rows, rows], axis=3)


def _attn_kernel(lo_of, hi_of, q_ref, k0_ref, k1_ref, k2_ref, v0_ref, v1_ref, v2_ref, t_ref, o_ref, b_ref):
    i = pl.program_id(1)
    lo = lo_of(i)
    hi = hi_of(i)
    n_pairs = q_ref.shape[0]
    lane = lax.broadcasted_iota(jnp.int32, (GRID_W, LANES), 1)

    @pl.when((i == lo) | (i == lo + 1) | (i == hi - 1))
    def _():
        case = jnp.where(i == lo, 0, jnp.where(i == hi - 1, 2, 1))
        for qr in range(Q_ROWS):
            first = jnp.where(case == 0, 0, jnp.where(case == 1, qr, Q_ROWS))

            def entry(j):
                ok = (j >= first) & (j < first + WIN_ROWS)
                return jnp.where(ok, j - case * Q_ROWS - qr + (WIN_ROWS - 1), MASKED_ROW)

            for h in range(2 * n_pairs):
                r0 = (h % 2) * Q_TOK + qr * GRID_W
                for j in range(0, K_ROWS, 2):
                    left = t_ref[h, entry(j)]
                    right = t_ref[h, entry(j + 1)]
                    b_ref[h // 2, r0:r0 + GRID_W, j * GRID_W:(j + 2) * GRID_W] = jnp.where(lane < GRID_W, left, right)

    qlane = lax.broadcasted_iota(jnp.int32, (Q_TOK, LANES), 1)
    for p in range(n_pairs):
        q = q_ref[p]
        zero = jnp.zeros_like(q)
        q2 = jnp.concatenate([jnp.where(qlane < HEAD_DIM, q, zero), jnp.where(qlane >= HEAD_DIM, q, zero)], axis=0)
        k = jnp.concatenate([k0_ref[p], k1_ref[p], k2_ref[p]], axis=0)
        v = jnp.concatenate([v0_ref[p], v1_ref[p], v2_ref[p]], axis=0)
        s = lax.dot_general(q2, k, (((1,), (1,)), ((), ())), preferred_element_type=F32)
        s = s * (HEAD_DIM ** -0.5) + b_ref[p]
        e = jnp.exp(s - jnp.max(s, axis=-1, keepdims=True))
        denom = jnp.sum(e, axis=-1, keepdims=True)
        o2 = jnp.dot(e.astype(BF16), v, preferred_element_type=F32) / denom
        o_ref[p] = jnp.where(qlane < HEAD_DIM, o2[:Q_TOK], o2[Q_TOK:]).astype(BF16)


def _attention(qkv, bias_rows, seg_groups, *, pairs_per_step):
    n_pairs = qkv.shape[0] // 3
    t = qkv.shape[1]
    pb = pairs_per_step
    assert n_pairs % pb == 0
    lo_of = lambda i: _seg_select(i, seg_groups, seg_groups[:-1])
    hi_of = lambda i: _seg_select(i, seg_groups, seg_groups[1:])
    window = lambda i: jnp.clip(i - 1, lo_of(i), hi_of(i) - K_ROWS // Q_ROWS)

    blk = (pb, Q_TOK, LANES)
    kv_specs = [pl.BlockSpec(blk, functools.partial(lambda p, i, off, j: (off + p, window(i) + j, 0), off=off, j=j))
                for off in (n_pairs // pb, 2 * n_pairs // pb) for j in range(K_ROWS // Q_ROWS)]
    tab_blk = (2 * pb,) + bias_rows.shape[1:]
    pipelined = 8 * _nbytes(blk, BF16) + _nbytes(tab_blk, F32)
    resident = _nbytes((pb, 2 * Q_TOK, K_TOK), F32) + 3 * pb * _nbytes((2 * Q_TOK, K_TOK), F32)
    return pl.pallas_call(
        functools.partial(_attn_kernel, lo_of, hi_of),
        out_shape=jax.ShapeDtypeStruct((n_pairs, t, LANES), BF16),
        grid=(n_pairs // pb, t // Q_TOK),
        in_specs=[pl.BlockSpec(blk, lambda p, i: (p, i, 0))] + kv_specs + [
            pl.BlockSpec(tab_blk, lambda p, i: (p, 0, 0, 0))],
        out_specs=pl.BlockSpec(blk, lambda p, i: (p, i, 0)),
        scratch_shapes=[pltpu.VMEM((pb, 2 * Q_TOK, K_TOK), F32)],
        compiler_params=_params(("arbitrary", "arbitrary"), pipelined, resident),
        name="nbr_attention",
    )(qkv, *([qkv] * 6), bias_rows)


def _conv_kernel(first_tiles, last_tiles, prev_ref, cur_ref, next_ref, w_ref, cb_ref, lg_ref, lb_ref,
                 o_ref, buf_ref, sh_ref, y_ref):
    i = pl.program_id(0)
    tm, c = cur_ref.shape
    is_first = functools.reduce(jnp.logical_or, [i == s for s in first_tiles])
    is_last = functools.reduce(jnp.logical_or, [i == s for s in last_tiles])
    buf_ref[0:CONV_HALO, :] = jnp.where(is_first, 0.0, prev_ref[...])
    buf_ref[CONV_HALO:CONV_HALO + tm, :] = cur_ref[...]
    buf_ref[CONV_HALO + tm:, :] = jnp.where(is_last, 0.0, next_ref[...])
    chunk = 64
    base = CONV_HALO - CONV_K // 2
    n_sh = sh_ref.shape[1]
    for l0 in range(0, c, LANES):
        lanes = slice(l0, l0 + LANES)
        for o in range(1, F32_SUBLANES):
            sh_ref[o] = buf_ref[o:o + n_sh, lanes]
        def chunk_body(ci, carry, lanes=lanes):
            t0 = pl.multiple_of(ci * chunk, chunk)
            acc = jnp.broadcast_to(cb_ref[:, lanes], (chunk, LANES))
            for k in range(CONV_K):
                o = (base + k) % F32_SUBLANES
                rows = pl.ds(t0 + (base + k - o), chunk)
                src = buf_ref[rows, lanes] if o == 0 else sh_ref[o, rows, :]
                acc = acc + w_ref[k:k + 1, lanes] * src
            y_ref[pl.ds(t0, chunk), lanes] = acc
            return carry

        lax.fori_loop(0, tm // chunk, chunk_body, 0)
    y = y_ref[...]
    mu = jnp.mean(y, axis=-1, keepdims=True)
    yc = y - mu
    var = jnp.mean(yc * yc, axis=-1, keepdims=True)
    z = yc * lax.rsqrt(var + NORM_EPS) * lg_ref[...] + lb_ref[...]
    o_ref[...] = (z * jax.nn.sigmoid(z)).astype(BF16)


def _conv_module(c, conv_w, conv_b, ln_g, ln_b, seq_starts, seq_ends, *, tm):
    t, width = c.shape
    first_tiles = tuple(s // tm for s in seq_starts)
    last_tiles = tuple(e // tm - 1 for e in seq_ends)
    hb = tm // CONV_HALO
    n_hb = t // CONV_HALO
    pipelined = _nbytes((tm + 2 * CONV_HALO, width), F32) + _nbytes((tm, width), BF16)
    resident = _nbytes((2 * tm + 2 * CONV_HALO, width), F32) + 3 * _nbytes((tm, width), F32)
    row = lambda a: a.reshape(1, width)
    return pl.pallas_call(
        functools.partial(_conv_kernel, first_tiles, last_tiles),
        out_shape=jax.ShapeDtypeStruct((t, width), BF16),
        grid=(t // tm,),
        in_specs=[
            pl.BlockSpec((CONV_HALO, width), lambda i: (jnp.maximum(i * hb - 1, 0), 0)),
            pl.BlockSpec((tm, width), lambda i: (i, 0)),
            pl.BlockSpec((CONV_HALO, width), lambda i: (jnp.minimum((i + 1) * hb, n_hb - 1), 0)),
            pl.BlockSpec((CONV_K, width), lambda i: (0, 0)),
            pl.BlockSpec((1, width), lambda i: (0, 0)),
            pl.BlockSpec((1, width), lambda i: (0, 0)),
            pl.BlockSpec((1, width), lambda i: (0, 0)),
        ],
        out_specs=pl.BlockSpec((tm, width), lambda i: (i, 0)),
        scratch_shapes=[pltpu.VMEM((tm + 2 * CONV_HALO, width), F32),
                        pltpu.VMEM((F32_SUBLANES, tm + 2 * CONV_HALO - F32_SUBLANES, LANES), F32),
                        pltpu.VMEM((tm, width), F32)],
        compiler_params=_params(("parallel",), pipelined, resident),
        name="conv_module",
    )(c, c, c, conv_w, row(conv_b), row(ln_g), row(ln_b))


def _outproj_kernel(x_ref, att_ref, cc_ref, w1_ref, w2_ref, o_ref):
    att = jnp.concatenate([att_ref[p] for p in range(att_ref.shape[0])], axis=1)
    acc = jnp.dot(att, w1_ref[...], preferred_element_type=F32)
    acc = acc + jnp.dot(cc_ref[...], w2_ref[...], preferred_element_type=F32)
    o_ref[...] = x_ref[...] + acc


def _outproj(x, att, cc, w_out, *, tm, tn):
    t, d = x.shape
    n_pairs = att.shape[0]
    wa = n_pairs * LANES
    wb = cc.shape[1]
    assert wa == wb and wa + wb == w_out.shape[0]
    pipelined = (2 * _nbytes((tm, tn), F32) + _nbytes((tm, wa), BF16) + _nbytes((tm, wb), BF16)
                 + _nbytes((wa + wb, tn), BF16))
    resident = 2 * _nbytes((tm, tn), F32) + _nbytes((tm, wa), BF16)
    return pl.pallas_call(
        _outproj_kernel,
        out_shape=jax.ShapeDtypeStruct((t, d), F32),
        grid=(t // tm, d // tn),
        in_specs=[
            pl.BlockSpec((tm, tn), lambda i, j: (i, j)),
            pl.BlockSpec((n_pairs, tm, LANES), lambda i, j: (0, i, 0)),
            pl.BlockSpec((tm, wb), lambda i, j: (i, 0)),
            pl.BlockSpec((wa, tn), lambda i, j: (0, j)),
            pl.BlockSpec((wb, tn), lambda i, j: (1, j)),
        ],
        out_specs=pl.BlockSpec((tm, tn), lambda i, j: (i, j)),
        compiler_params=_params(("parallel", "arbitrary"), pipelined, resident),
        name="attn_conv_out_proj",
    )(x, att, cc, w_out, w_out)


def _unit_circle(num, den):
    ang = (num % den).astype(F32) * F32(2.0 * np.pi / den)
    return jnp.cos(ang), jnp.sin(ang)


def _dft_cos_sin(n):
    idx = jnp.arange(n, dtype=jnp.int32)
    return _unit_circle(idx[:, None] * idx[None, :], n)


def _deinterleave_kernel(x_ref, g_ref, o_ref, hs_ref):
    tm, d = x_ref.shape
    radix = o_ref.shape[1]
    h = _rms(x_ref[...], g_ref[...])
    for s in range(d // LANES):
        hs_ref[s] = h[:, s * LANES:(s + 1) * LANES]
    for n1 in range(radix):
        for s in range(d // LANES):
            o_ref[0, n1, :, s * LANES:(s + 1) * LANES] = hs_ref[s, pl.ds(n1, tm // radix, stride=radix), :].astype(BF16)


def _deinterleave(x, gain, tok0, n_seq, s, *, tm):
    d = x.shape[1]
    tiles = s // tm
    blk0 = tok0 // tm
    assert blk0 * tm == tok0 and tiles * tm == s and (tm // DFT_RADIX) % BF16_SUBLANES == 0
    pipelined = _nbytes((tm, d), F32) + _nbytes((tm, d), BF16)
    resident = 2 * _nbytes((tm, d), F32)
    return pl.pallas_call(
        _deinterleave_kernel,
        out_shape=jax.ShapeDtypeStruct((n_seq, DFT_RADIX, s // DFT_RADIX, d), BF16),
        grid=(n_seq, tiles),
        in_specs=[pl.BlockSpec((tm, d), lambda b, i: (blk0 + b * tiles + i, 0)),
                  pl.BlockSpec((1, d), lambda b, i: (0, 0))],
        out_specs=pl.BlockSpec((1, DFT_RADIX, tm // DFT_RADIX, d), lambda b, i: (b, 0, i, 0)),
        scratch_shapes=[pltpu.VMEM((d // LANES, tm, LANES), F32)],
        compiler_params=_params(("parallel", "parallel"), pipelined, resident),
        name="fourier_deinterleave",
    )(x, gain.reshape(1, d))


def _dft_dense_kernel(fc_ref, fs_ref, tw_ref, h_ref, o_ref):
    h = h_ref[0, 0]
    yc = jnp.dot(fc_ref[...], h, preferred_element_type=F32)
    ys = jnp.dot(fs_ref[...], h, preferred_element_type=F32)
    tc = tw_ref[0, :, 0:1]
    ts = tw_ref[0, :, 1:2]
    o_ref[0, 0, 0] = (tc * yc - ts * ys).astype(BF16)
    o_ref[0, 0, 1] = (-(tc * ys + ts * yc)).astype(BF16)


def _dft_dense(h, fc, fs, tw, *, tr, tc):
    n_seq, radix, s2, d = h.shape
    pipelined = 2 * _nbytes((tr, s2), BF16) + _nbytes((s2, tc), BF16) + _nbytes((2, tr, tc), BF16)
    resident = 4 * _nbytes((tr, tc), F32)
    return pl.pallas_call(
        _dft_dense_kernel,
        out_shape=jax.ShapeDtypeStruct((n_seq, radix, 2, s2, d), BF16),
        grid=(n_seq, radix, d // tc, s2 // tr),
        in_specs=[
            pl.BlockSpec((tr, s2), lambda b, n, c, r: (r, 0)),
            pl.BlockSpec((tr, s2), lambda b, n, c, r: (r, 0)),
            pl.BlockSpec((1, tr, 2), lambda b, n, c, r: (n, r, 0)),
            pl.BlockSpec((1, 1, s2, tc), lambda b, n, c, r: (b, n, 0, c)),
        ],
        out_specs=pl.BlockSpec((1, 1, 2, tr, tc), lambda b, n, c, r: (b, n, 0, r, c)),
        compiler_params=_params(("parallel", "parallel", "parallel", "arbitrary"), pipelined, resident),
        name="fourier_dense_dft",
    )(fc, fs, tw, h)


def _fft(z):
    n = len(z)
    if n == 1:
        return z
    even, odd = _fft(z[0::2]), _fft(z[1::2])
    out = [None] * n
    for k in range(n // 2):
        o_r, o_i = odd[k]
        if k == 0:
            t_r, t_i = o_r, o_i
        elif 4 * k == n:
            t_r, t_i = o_i, -o_r
        else:
            c, s = math.cos(2 * math.pi * k / n), math.sin(2 * math.pi * k / n)
            t_r, t_i = c * o_r + s * o_i, c * o_i - s * o_r
        e_r, e_i = even[k]
        out[k] = (e_r + t_r, e_i + t_i)
        out[k + n // 2] = (e_r - t_r, e_i - t_i)
    return out


def _radix_kernel(z_ref, pr_ref, pi_ref):
    radix, tq, tc = z_ref.shape[1], z_ref.shape[3], z_ref.shape[4]

    def body(g, carry):
        rows = pl.ds(pl.multiple_of(g * BF16_SUBLANES, BF16_SUBLANES), BF16_SUBLANES)
        for l0 in range(0, tc, LANES):
            lanes = slice(l0, l0 + LANES)
            z = [(z_ref[0, n1, 0, rows, lanes].astype(F32), z_ref[0, n1, 1, rows, lanes].astype(F32))
                 for n1 in range(radix)]
            for k1, (p_r, p_i) in enumerate(_fft(z)):
                pr_ref[0, k1, rows, lanes] = p_r.astype(BF16)
                pi_ref[0, k1, rows, lanes] = p_i.astype(BF16)
        return carry

    lax.fori_loop(0, tq // BF16_SUBLANES, body, 0)


def _radix_combine(z, *, tq, tc):
    n_seq, radix, _, s2, d = z.shape
    out = jax.ShapeDtypeStruct((n_seq, radix, s2, d), BF16)
    oblk = pl.BlockSpec((1, radix, tq, tc), lambda b, q, c: (b, 0, q, c))
    pipelined = 2 * _nbytes((radix, 2, tq, tc), BF16)
    return pl.pallas_call(
        _radix_kernel,
        out_shape=(out, out),
        grid=(n_seq, s2 // tq, d // tc),
        in_specs=[pl.BlockSpec((1, radix, 2, tq, tc), lambda b, q, c: (b, 0, 0, q, c))],
        out_specs=(oblk, oblk),
        compiler_params=_params(("parallel", "parallel", "parallel"), pipelined),
        name="fourier_radix_combine",
    )(z)


def _fourier_out_kernel(scale, x_ref, pr_ref, pi_ref, d_ref, w_ref, o_ref, y_ref):
    @pl.when(pl.program_id(1) == 0)
    def _():
        gw = d_ref.shape[1]
        for q in range(pr_ref.shape[1] // gw):
            cols = slice(q * gw, (q + 1) * gw)
            lhs = jnp.concatenate([pr_ref[:, cols], pi_ref[:, cols]], axis=1)
            y_ref[:, cols] = (jnp.dot(lhs, d_ref[...], preferred_element_type=F32) * scale).astype(BF16)

    o_ref[...] = x_ref[...] + jnp.dot(y_ref[...], w_ref[...], preferred_element_type=F32)


def _fourier_out(x, p_r, p_i, dmat, w, scale, tok0, *, tm, tn):
    t, d = x.shape
    n_tok = p_r.shape[0]
    blk0 = tok0 // tm
    assert blk0 * tm == tok0 and n_tok % tm == 0
    pipelined = (2 * _nbytes((tm, tn), F32) + 2 * _nbytes((tm, d), BF16) + _nbytes(dmat.shape, BF16)
                 + _nbytes((d, tn), BF16))
    resident = _nbytes((tm, d), BF16) + 2 * _nbytes((tm, tn), F32)
    return pl.pallas_call(
        functools.partial(_fourier_out_kernel, scale),
        out_shape=jax.ShapeDtypeStruct((t, d), F32),
        grid=(n_tok // tm, d // tn),
        in_specs=[
            pl.BlockSpec((tm, tn), lambda i, j: (blk0 + i, j)),
            pl.BlockSpec((tm, d), lambda i, j: (i, 0)),
            pl.BlockSpec((tm, d), lambda i, j: (i, 0)),
            pl.BlockSpec(dmat.shape, lambda i, j: (0, 0)),
            pl.BlockSpec((d, tn), lambda i, j: (0, j)),
        ],
        out_specs=pl.BlockSpec((tm, tn), lambda i, j: (blk0 + i, j)),
        scratch_shapes=[pltpu.VMEM((tm, d), BF16)],
        input_output_aliases={0: 0},
        compiler_params=_params(("parallel", "arbitrary"), pipelined, resident),
        name="fourier_out_proj",
    )(x, p_r, p_i, dmat, w)


def _fourier_mix(x, gain, w_out, groups, *, tm, tn):
    d = x.shape[1]
    gw = d // FOURIER_GROUPS
    cc, sc = _dft_cos_sin(gw)
    dmat = jnp.concatenate([cc, sc], axis=0).astype(BF16)
    mixed = []
    for tok0, n_seq, s in groups:
        s2 = s // DFT_RADIX
        fc, fs = _dft_cos_sin(s2)
        n1 = jnp.arange(DFT_RADIX, dtype=jnp.int32)[:, None]
        k2 = jnp.arange(s2, dtype=jnp.int32)[None, :]
        tw = jnp.stack(_unit_circle(n1 * k2, s), axis=2)
        h = _deinterleave(x, gain, tok0, n_seq, s, tm=tm)
        z = _dft_dense(h, fc.astype(BF16), fs.astype(BF16), tw, tr=min(s2, 512), tc=_tile(d, 1024))
        p_r, p_i = _radix_combine(z, tq=min(s2, 128), tc=_tile(d, 256))
        mixed.append((p_r.reshape(n_seq * s, d), p_i.reshape(n_seq * s, d), float((s * gw) ** -0.5), tok0))
    for p_r, p_i, scale, tok0 in mixed:
        x = _fourier_out(x, p_r, p_i, dmat, w_out, scale, tok0, tm=tm, tn=tn)
    return x


def _trunk(xs, seqs, out_ranges, p):
    t, d = xs.shape
    tm = _tile(math.gcd(*[n for _, n in seqs]), 1024)
    depth = p["ffn1_norm"].shape[0]
    a_width = p["ab_rpb"].shape[1] * HEAD_DIM
    b_width = d - a_width
    seq_starts = [s for s, _ in seqs]
    seq_ends = [s + n for s, n in seqs]
    seg_groups = [s // Q_TOK for s in seq_starts] + [seq_ends[-1] // Q_TOK]
    for s, n in seqs:
        assert s % tm == 0 and n % tm == 0 and n >= K_TOK
    groups = []
    for s, n in seqs:
        if groups and groups[-1][2] == n and groups[-1][0] + groups[-1][1] * n == s:
            groups[-1] = (groups[-1][0], groups[-1][1] + 1, n)
        else:
            groups.append((s, 1, n))
    bf = lambda a: a.astype(BF16)
    tf = _tile(p["ffn1_w_gate"].shape[2], 512)
    tn = _tile(d, 1024)
    x = xs
    for i in range(depth):
        x = _ffn(x, p["ffn1_norm"][i], bf(p["ffn1_w_gate"][i]), bf(p["ffn1_w_up"][i]), bf(p["ffn1_w_down"][i]),
                 tm=tm, tf=tf)
        j = i // 2
        if i % 2 == 0:
            w_in = bf(p["ab_w_in"][j])
            qkv = _qkv(x, p["mix_norm"][i], w_in, 3 * a_width, tm=tm, tn=_tile(a_width, 1024))
            c = _glu(x, p["mix_norm"][i], w_in, 3 * a_width, 3 * a_width + b_width, b_width, tm=tm, tn=_tile(b_width, 512))
            att = _attention(qkv, _attn_bias_rows(p["ab_rpb"][j]), seg_groups,
                             pairs_per_step=math.gcd(a_width // LANES, 4))
            cc = _conv_module(c, p["ab_conv_w"][j], p["ab_conv_b"][j], p["ab_ln_g"][j], p["ab_ln_b"][j],
                              seq_starts, seq_ends, tm=min(tm, 512))
            x = _outproj(x, att, cc, bf(p["ab_w_out"][j]), tm=tm, tn=tn)
        else:
            x = _fourier_mix(x, p["mix_norm"][i], bf(p["c_w_out"][j]), groups, tm=tm, tn=tn)
        ffn2 = functools.partial(_ffn, x, p["ffn2_norm"][i], bf(p["ffn2_w_gate"][i]), bf(p["ffn2_w_up"][i]),
                                 bf(p["ffn2_w_down"][i]), tm=tm, tf=tf)
        if i < depth - 1:
            x = ffn2()
    return [ffn2(out_gain=p["final_norm"], tok0=tok0, n_tok=n_tok) for tok0, n_tok in out_ranges]


def kernel(x_prompt, x_sample, ffn1_norm, ffn1_w_gate, ffn1_w_up, ffn1_w_down, mix_norm, ab_w_in, ab_rpb, ab_conv_w, ab_conv_b, ab_ln_g, ab_ln_b, ab_w_out, c_w_out, ffn2_norm, ffn2_w_gate, ffn2_w_up, ffn2_w_down, final_norm):
    params = dict(ffn1_norm=ffn1_norm, ffn1_w_gate=ffn1_w_gate, ffn1_w_up=ffn1_w_up, ffn1_w_down=ffn1_w_down,
                  mix_norm=mix_norm, ab_w_in=ab_w_in, ab_rpb=ab_rpb, ab_conv_w=ab_conv_w, ab_conv_b=ab_conv_b,
                  ab_ln_g=ab_ln_g, ab_ln_b=ab_ln_b, ab_w_out=ab_w_out, c_w_out=c_w_out, ffn2_norm=ffn2_norm,
                  ffn2_w_gate=ffn2_w_gate, ffn2_w_up=ffn2_w_up, ffn2_w_down=ffn2_w_down, final_norm=final_norm)
    bp, sp, d = x_prompt.shape
    bs, ss, _ = x_sample.shape
    n_s, n_p = bs * ss, bp * sp
    xs = jnp.concatenate([x_sample.reshape(n_s, d), x_prompt.reshape(n_p, d)], axis=0)
    seqs = [(b * ss, ss) for b in range(bs)] + [(n_s + b * sp, sp) for b in range(bp)]
    y_sample, y_prompt = _trunk(xs, seqs, [(0, n_s), (n_s, n_p)], params)
    return (y_prompt.reshape(bp, sp, d), y_sample.reshape(bs, ss, d))
```

```python
import functools
import math

import numpy as np
import jax
import jax.numpy as jnp
from jax import lax
from jax.experimental import pallas as pl
from jax.experimental.pallas import tpu as pltpu

F32 = jnp.float32
BF16 = jnp.bfloat16

HEAD_DIM = 64
GRID_W = 64
WIN_ROWS = 8
WIN_COLS = 16
CONV_K = 31
FOURIER_GROUPS = 4
NORM_EPS = 1e-6

LANES = 128
F32_SUBLANES = 8
BF16_SUBLANES = 16
VMEM_BUDGET_BYTES = 60000 * 1024

Q_ROWS = 4
K_ROWS = 12
Q_TOK = Q_ROWS * GRID_W
K_TOK = K_ROWS * GRID_W
MASKED = -1e30
DFT_RADIX = 16
CONV_HALO = 16


def _vmem_limit(pipelined_bytes, resident_bytes=0):
    return int(min(VMEM_BUDGET_BYTES, 2 * pipelined_bytes + resident_bytes + (4 << 20)))


def _nbytes(shape, dtype):
    return math.prod(shape) * jnp.dtype(dtype).itemsize


def _params(semantics, pipelined_bytes, resident_bytes=0):
    return pltpu.CompilerParams(dimension_semantics=semantics,
                                vmem_limit_bytes=_vmem_limit(pipelined_bytes, resident_bytes))


def _rms(x, gain):
    return x * lax.rsqrt(jnp.mean(x * x, axis=-1, keepdims=True) + NORM_EPS) * gain


def _tile(n, preferred):
    best = LANES
    for c in range(LANES, min(n, preferred) + 1, LANES):
        if n % c == 0:
            best = c
    assert n % best == 0
    return best


def _seg_select(i, bounds, values):
    out = values[-1]
    for s in range(len(values) - 2, -1, -1):
        out = jnp.where(i < bounds[s + 1], values[s], out)
    return out


def _ffn_kernel(out_norm, x_ref, g_ref, wg_ref, wu_ref, wd_ref, og_ref, *rest):
    o_ref, h_ref = rest[-2:]
    j = pl.program_id(1)

    def mlp(h):
        gate = jnp.dot(h, wg_ref[...], preferred_element_type=F32)
        up = jnp.dot(h, wu_ref[...], preferred_element_type=F32)
        act = (gate * jax.nn.sigmoid(gate)) * (0.5 * up)
        return jnp.dot(act.astype(BF16), wd_ref[...], preferred_element_type=F32)

    @pl.when(j == 0)
    def _():
        x = x_ref[...]
        h = _rms(x, g_ref[...]).astype(BF16)
        h_ref[...] = h
        o_ref[...] = x + mlp(h)

    @pl.when(j != 0)
    def _():
        o_ref[...] += mlp(h_ref[...])

    if out_norm:
        @pl.when(j == pl.num_programs(1) - 1)
        def _():
            o_ref[...] = _rms(o_ref[...], og_ref[...])


def _ffn(x, gain, w_gate, w_up, w_down, *, tm, tf, out_gain=None, tok0=0, n_tok=None, out_tok0=0, out_total=None,
         out_buf=None):
    t, d = x.shape
    n_tok = t if n_tok is None else n_tok
    out_total = n_tok if out_total is None else out_total
    f = w_gate.shape[1]
    blk0, oblk0 = tok0 // tm, out_tok0 // tm
    assert blk0 * tm == tok0 and oblk0 * tm == out_tok0 and n_tok % tm == 0
    pipelined = (_nbytes((tm, d), F32) * 2 + 2 * _nbytes((d, tf), BF16) + _nbytes((tf, d), BF16))
    resident = _nbytes((tm, d), BF16) + 3 * _nbytes((tm, tf), F32)
    og = gain if out_gain is None else out_gain
    in_specs = [
        pl.BlockSpec((tm, d), lambda i, j: (blk0 + i, 0)),
        pl.BlockSpec((1, d), lambda i, j: (0, 0)),
        pl.BlockSpec((d, tf), lambda i, j: (0, j)),
        pl.BlockSpec((d, tf), lambda i, j: (0, j)),
        pl.BlockSpec((tf, d), lambda i, j: (j, 0)),
        pl.BlockSpec((1, d), lambda i, j: (0, 0)),
    ]
    args = [x, gain.reshape(1, d), w_gate, w_up, w_down, og.reshape(1, d)]
    aliases = {}
    if out_buf is not None:
        assert out_buf.shape == (out_total, d)
        in_specs.append(pl.BlockSpec(memory_space=pl.ANY))
        args.append(out_buf)
        aliases = {len(args) - 1: 0}
    return pl.pallas_call(
        functools.partial(_ffn_kernel, out_gain is not None),
        out_shape=jax.ShapeDtypeStruct((out_total, d), F32),
        grid=(n_tok // tm, f // tf),
        in_specs=in_specs,
        out_specs=pl.BlockSpec((tm, d), lambda i, j: (oblk0 + i, 0)),
        scratch_shapes=[pltpu.VMEM((tm, d), BF16)],
        input_output_aliases=aliases,
        compiler_params=_params(("parallel", "arbitrary"), pipelined, resident),
        name="ffn",
    )(*args)


def _qkv_kernel(x_ref, g_ref, w_ref, o_ref, h_ref):
    def project(h):
        r = jnp.dot(h, w_ref[...], preferred_element_type=F32)
        for p in range(o_ref.shape[0]):
            o_ref[p] = r[:, p * LANES:(p + 1) * LANES].astype(BF16)

    @pl.when(pl.program_id(1) == 0)
    def _():
        h = _rms(x_ref[...], g_ref[...]).astype(BF16)
        h_ref[...] = h
        project(h)

    @pl.when(pl.program_id(1) != 0)
    def _():
        project(h_ref[...])


def _qkv(x, gain, w_in, n_cols, *, tm, tn):
    t, d = x.shape
    pipelined = _nbytes((tm, d), F32) + _nbytes((d, tn), BF16) + _nbytes((tm, tn), BF16)
    resident = _nbytes((tm, d), BF16) + _nbytes((tm, tn), F32)
    return pl.pallas_call(
        _qkv_kernel,
        out_shape=jax.ShapeDtypeStruct((n_cols // LANES, t, LANES), BF16),
        grid=(t // tm, n_cols // tn),
        in_specs=[
            pl.BlockSpec((tm, d), lambda i, j: (i, 0)),
            pl.BlockSpec((1, d), lambda i, j: (0, 0)),
            pl.BlockSpec((d, tn), lambda i, j: (0, j)),
        ],
        out_specs=pl.BlockSpec((tn // LANES, tm, LANES), lambda i, j: (j, i, 0)),
        scratch_shapes=[pltpu.VMEM((tm, d), BF16)],
        compiler_params=_params(("parallel", "arbitrary"), pipelined, resident),
        name="qkv_proj",
    )(x, gain.reshape(1, d), w_in)


def _glu_kernel(x_ref, g_ref, wa_ref, wg_ref, o_ref, h_ref):
    def project(h):
        a = jnp.dot(h, wa_ref[...], preferred_element_type=F32)
        g = jnp.dot(h, wg_ref[...], preferred_element_type=F32)
        o_ref[...] = a * jax.nn.sigmoid(g)

    @pl.when(pl.program_id(1) == 0)
    def _():
        h = _rms(x_ref[...], g_ref[...]).astype(BF16)
        h_ref[...] = h
        project(h)

    @pl.when(pl.program_id(1) != 0)
    def _():
        project(h_ref[...])


def _glu(x, gain, w_in, a_col0, g_col0, width, *, tm, tn):
    t, d = x.shape
    pipelined = _nbytes((tm, d), F32) + 2 * _nbytes((d, tn), BF16) + _nbytes((tm, tn), F32)
    resident = _nbytes((tm, d), BF16) + 2 * _nbytes((tm, tn), F32)
    return pl.pallas_call(
        _glu_kernel,
        out_shape=jax.ShapeDtypeStruct((t, width), F32),
        grid=(t // tm, width // tn),
        in_specs=[
            pl.BlockSpec((tm, d), lambda i, j: (i, 0)),
            pl.BlockSpec((1, d), lambda i, j: (0, 0)),
            pl.BlockSpec((d, tn), lambda i, j: (0, a_col0 // tn + j)),
            pl.BlockSpec((d, tn), lambda i, j: (0, g_col0 // tn + j)),
        ],
        out_specs=pl.BlockSpec((tm, tn), lambda i, j: (i, j)),
        scratch_shapes=[pltpu.VMEM((tm, d), BF16)],
        compiler_params=_params(("parallel", "arbitrary"), pipelined, resident),
        name="glu_proj",
    )(x, gain.reshape(1, d), w_in, w_in)


N_ROW_OFFSETS = 2 * WIN_ROWS - 1
MASKED_ROW = N_ROW_OFFSETS


def _attn_bias_rows(rpb):
    n_heads = rpb.shape[0]
    pad = GRID_W - WIN_COLS
    padded = jnp.pad(rpb.astype(F32), ((0, 0), (0, 0), (pad, pad)))
    rows = jnp.stack([padded[:, :, GRID_W - 1 - c:2 * GRID_W - 1 - c] for c in range(GRID_W)], axis=2)
    c = np.arange(GRID_W)[:, None]
    kc = np.arange(GRID_W)[None, :]
    c0 = np.clip(c - WIN_COLS // 2, 0, GRID_W - WIN_COLS)
    c_ok = (kc >= c0) & (kc < c0 + WIN_COLS)
    rows = jnp.where(c_ok[None, None], rows, MASKED)
    rows = jnp.concatenate([rows, jnp.full((n_heads, 1, GRID_W, GRID_W), MASKED, F32)], axis=1)
    return jnp.concatenate([rows, rows], axis=3)


def _attn_kernel(lo_of, hi_of, q_ref, k0_ref, k1_ref, k2_ref, v0_ref, v1_ref, v2_ref, t_ref, o_ref, b_ref):
    i = pl.program_id(1)
    lo = lo_of(i)
    hi = hi_of(i)
    n_pairs = q_ref.shape[0]
    lane = lax.broadcasted_iota(jnp.int32, (GRID_W, LANES), 1)

    @pl.when((i == lo) | (i == lo + 1) | (i == hi - 1))
    def _():
        case = jnp.where(i == lo, 0, jnp.where(i == hi - 1, 2, 1))
        for qr in range(Q_ROWS):
            first = jnp.where(case == 0, 0, jnp.where(case == 1, qr, Q_ROWS))

            def entry(j):
                ok = (j >= first) & (j < first + WIN_ROWS)
                return jnp.where(ok, j - case * Q_ROWS - qr + (WIN_ROWS - 1), MASKED_ROW)

            for h in range(2 * n_pairs):
                r0 = (h % 2) * Q_TOK + qr * GRID_W
                for j in range(0, K_ROWS, 2):
                    left = t_ref[h, entry(j)]
                    right = t_ref[h, entry(j + 1)]
                    b_ref[h // 2, r0:r0 + GRID_W, j * GRID_W:(j + 2) * GRID_W] = jnp.where(lane < GRID_W, left, right)

    qlane = lax.broadcasted_iota(jnp.int32, (Q_TOK, LANES), 1)
    for p in range(n_pairs):
        q = q_ref[p]
        zero = jnp.zeros_like(q)
        q2 = jnp.concatenate([jnp.where(qlane < HEAD_DIM, q, zero), jnp.where(qlane >= HEAD_DIM, q, zero)], axis=0)
        k = jnp.concatenate([k0_ref[p], k1_ref[p], k2_ref[p]], axis=0)
        v = jnp.concatenate([v0_ref[p], v1_ref[p], v2_ref[p]], axis=0)
        s = lax.dot_general(q2, k, (((1,), (1,)), ((), ())), preferred_element_type=F32)
        s = s * (HEAD_DIM ** -0.5) + b_ref[p]
        e = jnp.exp(s - jnp.max(s, axis=-1, keepdims=True))
        denom = jnp.sum(e, axis=-1, keepdims=True)
        o2 = jnp.dot(e.astype(BF16), v, preferred_element_type=F32) / denom
        o_ref[p] = jnp.where(qlane < HEAD_DIM, o2[:Q_TOK], o2[Q_TOK:]).astype(BF16)


def _attention(qkv, bias_rows, seg_groups, *, pairs_per_step):
    n_pairs = qkv.shape[0] // 3
    t = qkv.shape[1]
    pb = pairs_per_step
    assert n_pairs % pb == 0
    lo_of = lambda i: _seg_select(i, seg_groups, seg_groups[:-1])
    hi_of = lambda i: _seg_select(i, seg_groups, seg_groups[1:])
    window = lambda i: jnp.clip(i - 1, lo_of(i), hi_of(i) - K_ROWS // Q_ROWS)

    blk = (pb, Q_TOK, LANES)
    kv_specs = [pl.BlockSpec(blk, functools.partial(lambda p, i, off, j: (off + p, window(i) + j, 0), off=off, j=j))
                for off in (n_pairs // pb, 2 * n_pairs // pb) for j in range(K_ROWS // Q_ROWS)]
    tab_blk = (2 * pb,) + bias_rows.shape[1:]
    pipelined = 8 * _nbytes(blk, BF16) + _nbytes(tab_blk, F32)
    resident = _nbytes((pb, 2 * Q_TOK, K_TOK), F32) + 3 * pb * _nbytes((2 * Q_TOK, K_TOK), F32)
    return pl.pallas_call(
        functools.partial(_attn_kernel, lo_of, hi_of),
        out_shape=jax.ShapeDtypeStruct((n_pairs, t, LANES), BF16),
        grid=(n_pairs // pb, t // Q_TOK),
        in_specs=[pl.BlockSpec(blk, lambda p, i: (p, i, 0))] + kv_specs + [
            pl.BlockSpec(tab_blk, lambda p, i: (p, 0, 0, 0))],
        out_specs=pl.BlockSpec(blk, lambda p, i: (p, i, 0)),
        scratch_shapes=[pltpu.VMEM((pb, 2 * Q_TOK, K_TOK), F32)],
        compiler_params=_params(("arbitrary", "arbitrary"), pipelined, resident),
        name="nbr_attention",
    )(qkv, *([qkv] * 6), bias_rows)


def _conv_kernel(first_tiles, last_tiles, prev_ref, cur_ref, next_ref, w_ref, cb_ref, lg_ref, lb_ref,
                 o_ref, buf_ref, sh_ref, y_ref):
    i = pl.program_id(0)
    tm, c = cur_ref.shape
    is_first = functools.reduce(jnp.logical_or, [i == s for s in first_tiles])
    is_last = functools.reduce(jnp.logical_or, [i == s for s in last_tiles])
    buf_ref[0:CONV_HALO, :] = jnp.where(is_first, 0.0, prev_ref[...])
    buf_ref[CONV_HALO:CONV_HALO + tm, :] = cur_ref[...]
    buf_ref[CONV_HALO + tm:, :] = jnp.where(is_last, 0.0, next_ref[...])
    chunk = 64
    base = CONV_HALO - CONV_K // 2
    n_sh = sh_ref.shape[1]
    for l0 in range(0, c, LANES):
        lanes = slice(l0, l0 + LANES)
        for o in range(1, F32_SUBLANES):
            sh_ref[o] = buf_ref[o:o + n_sh, lanes]
        def chunk_body(ci, carry, lanes=lanes):
            t0 = pl.multiple_of(ci * chunk, chunk)
            acc = jnp.broadcast_to(cb_ref[:, lanes], (chunk, LANES))
            for k in range(CONV_K):
                o = (base + k) % F32_SUBLANES
                rows = pl.ds(t0 + (base + k - o), chunk)
                src = buf_ref[rows, lanes] if o == 0 else sh_ref[o, rows, :]
                acc = acc + w_ref[k:k + 1, lanes] * src
            y_ref[pl.ds(t0, chunk), lanes] = acc
            return carry

        lax.fori_loop(0, tm // chunk, chunk_body, 0)
    y = y_ref[...]
    mu = jnp.mean(y, axis=-1, keepdims=True)
    yc = y - mu
    var = jnp.mean(yc * yc, axis=-1, keepdims=True)
    z = yc * lax.rsqrt(var + NORM_EPS) * lg_ref[...] + lb_ref[...]
    o_ref[...] = (z * jax.nn.sigmoid(z)).astype(BF16)


def _conv_module(c, conv_w, conv_b, ln_g, ln_b, seq_starts, seq_ends, *, tm):
    t, width = c.shape
    first_tiles = tuple(s // tm for s in seq_starts)
    last_tiles = tuple(e // tm - 1 for e in seq_ends)
    hb = tm // CONV_HALO
    n_hb = t // CONV_HALO
    pipelined = _nbytes((tm + 2 * CONV_HALO, width), F32) + _nbytes((tm, width), BF16)
    resident = _nbytes((2 * tm + 2 * CONV_HALO, width), F32) + 3 * _nbytes((tm, width), F32)
    row = lambda a: a.reshape(1, width)
    return pl.pallas_call(
        functools.partial(_conv_kernel, first_tiles, last_tiles),
        out_shape=jax.ShapeDtypeStruct((t, width), BF16),
        grid=(t // tm,),
        in_specs=[
            pl.BlockSpec((CONV_HALO, width), lambda i: (jnp.maximum(i * hb - 1, 0), 0)),
            pl.BlockSpec((tm, width), lambda i: (i, 0)),
            pl.BlockSpec((CONV_HALO, width), lambda i: (jnp.minimum((i + 1) * hb, n_hb - 1), 0)),
            pl.BlockSpec((CONV_K, width), lambda i: (0, 0)),
            pl.BlockSpec((1, width), lambda i: (0, 0)),
            pl.BlockSpec((1, width), lambda i: (0, 0)),
            pl.BlockSpec((1, width), lambda i: (0, 0)),
        ],
        out_specs=pl.BlockSpec((tm, width), lambda i: (i, 0)),
        scratch_shapes=[pltpu.VMEM((tm + 2 * CONV_HALO, width), F32),
                        pltpu.VMEM((F32_SUBLANES, tm + 2 * CONV_HALO - F32_SUBLANES, LANES), F32),
                        pltpu.VMEM((tm, width), F32)],
        compiler_params=_params(("parallel",), pipelined, resident),
        name="conv_module",
    )(c, c, c, conv_w, row(conv_b), row(ln_g), row(ln_b))


def _outproj_kernel(x_ref, att_ref, cc_ref, w1_ref, w2_ref, o_ref):
    att = jnp.concatenate([att_ref[p] for p in range(att_ref.shape[0])], axis=1)
    acc = jnp.dot(att, w1_ref[...], preferred_element_type=F32)
    acc = acc + jnp.dot(cc_ref[...], w2_ref[...], preferred_element_type=F32)
    o_ref[...] = x_ref[...] + acc


def _outproj(x, att, cc, w_out, *, tm, tn):
    t, d = x.shape
    n_pairs = att.shape[0]
    wa = n_pairs * LANES
    wb = cc.shape[1]
    assert wa == wb and wa + wb == w_out.shape[0]
    pipelined = (2 * _nbytes((tm, tn), F32) + _nbytes((tm, wa), BF16) + _nbytes((tm, wb), BF16)
                 + _nbytes((wa + wb, tn), BF16))
    resident = 2 * _nbytes((tm, tn), F32) + _nbytes((tm, wa), BF16)
    return pl.pallas_call(
        _outproj_kernel,
        out_shape=jax.ShapeDtypeStruct((t, d), F32),
        grid=(t // tm, d // tn),
        in_specs=[
            pl.BlockSpec((tm, tn), lambda i, j: (i, j)),
            pl.BlockSpec((n_pairs, tm, LANES), lambda i, j: (0, i, 0)),
            pl.BlockSpec((tm, wb), lambda i, j: (i, 0)),
            pl.BlockSpec((wa, tn), lambda i, j: (0, j)),
            pl.BlockSpec((wb, tn), lambda i, j: (1, j)),
        ],
        out_specs=pl.BlockSpec((tm, tn), lambda i, j: (i, j)),
        compiler_params=_params(("parallel", "arbitrary"), pipelined, resident),
        name="attn_conv_out_proj",
    )(x, att, cc, w_out, w_out)


def _unit_circle(num, den):
    ang = (num % den).astype(F32) * F32(2.0 * np.pi / den)
    return jnp.cos(ang), jnp.sin(ang)


def _dft_cos_sin(n):
    idx = jnp.arange(n, dtype=jnp.int32)
    return _unit_circle(idx[:, None] * idx[None, :], n)


def _deinterleave_kernel(x_ref, g_ref, o_ref, hs_ref):
    tm, d = x_ref.shape
    radix = o_ref.shape[1]
    h = _rms(x_ref[...], g_ref[...])
    for s in range(d // LANES):
        hs_ref[s] = h[:, s * LANES:(s + 1) * LANES]
    for n1 in range(radix):
        for s in range(d // LANES):
            o_ref[0, n1, :, s * LANES:(s + 1) * LANES] = hs_ref[s, pl.ds(n1, tm // radix, stride=radix), :].astype(BF16)


def _deinterleave(x, gain, tok0, n_seq, s, *, tm):
    d = x.shape[1]
    tiles = s // tm
    blk0 = tok0 // tm
    assert blk0 * tm == tok0 and tiles * tm == s and (tm // DFT_RADIX) % BF16_SUBLANES == 0
    pipelined = _nbytes((tm, d), F32) + _nbytes((tm, d), BF16)
    resident = 2 * _nbytes((tm, d), F32)
    return pl.pallas_call(
        _deinterleave_kernel,
        out_shape=jax.ShapeDtypeStruct((n_seq, DFT_RADIX, s // DFT_RADIX, d), BF16),
        grid=(n_seq, tiles),
        in_specs=[pl.BlockSpec((tm, d), lambda b, i: (blk0 + b * tiles + i, 0)),
                  pl.BlockSpec((1, d), lambda b, i: (0, 0))],
        out_specs=pl.BlockSpec((1, DFT_RADIX, tm // DFT_RADIX, d), lambda b, i: (b, 0, i, 0)),
        scratch_shapes=[pltpu.VMEM((d // LANES, tm, LANES), F32)],
        compiler_params=_params(("parallel", "parallel"), pipelined, resident),
        name="fourier_deinterleave",
    )(x, gain.reshape(1, d))


def _dft_dense_kernel(fc_ref, fs_ref, tw_ref, h_ref, o_ref):
    h = h_ref[0, 0]
    yc = jnp.dot(fc_ref[...], h, preferred_element_type=F32)
    ys = jnp.dot(fs_ref[...], h, preferred_element_type=F32)
    tc = tw_ref[0, :, 0:1]
    ts = tw_ref[0, :, 1:2]
    o_ref[0, 0, 0] = (tc * yc - ts * ys).astype(BF16)
    o_ref[0, 0, 1] = (-(tc * ys + ts * yc)).astype(BF16)


def _dft_dense(h, fc, fs, tw, *, tr, tc):
    n_seq, radix, s2, d = h.shape
    pipelined = 2 * _nbytes((tr, s2), BF16) + _nbytes((s2, tc), BF16) + _nbytes((2, tr, tc), BF16)
    resident = 4 * _nbytes((tr, tc), F32)
    return pl.pallas_call(
        _dft_dense_kernel,
        out_shape=jax.ShapeDtypeStruct((n_seq, radix, 2, s2, d), BF16),
        grid=(n_seq, radix, d // tc, s2 // tr),
        in_specs=[
            pl.BlockSpec((tr, s2), lambda b, n, c, r: (r, 0)),
            pl.BlockSpec((tr, s2), lambda b, n, c, r: (r, 0)),
            pl.BlockSpec((1, tr, 2), lambda b, n, c, r: (n, r, 0)),
            pl.BlockSpec((1, 1, s2, tc), lambda b, n, c, r: (b, n, 0, c)),
        ],
        out_specs=pl.BlockSpec((1, 1, 2, tr, tc), lambda b, n, c, r: (b, n, 0, r, c)),
        compiler_params=_params(("parallel", "parallel", "parallel", "arbitrary"), pipelined, resident),
        name="fourier_dense_dft",
    )(fc, fs, tw, h)


def _fft(z):
    n = len(z)
    if n == 1:
        return z
    even, odd = _fft(z[0::2]), _fft(z[1::2])
    out = [None] * n
    for k in range(n // 2):
        o_r, o_i = odd[k]
        if k == 0:
            t_r, t_i = o_r, o_i
        elif 4 * k == n:
            t_r, t_i = o_i, -o_r
        else:
            c, s = math.cos(2 * math.pi * k / n), math.sin(2 * math.pi * k / n)
            t_r, t_i = c * o_r + s * o_i, c * o_i - s * o_r
        e_r, e_i = even[k]
        out[k] = (e_r + t_r, e_i + t_i)
        out[k + n // 2] = (e_r - t_r, e_i - t_i)
    return out


def _radix_kernel(z_ref, pr_ref, pi_ref):
    radix, tq, tc = z_ref.shape[1], z_ref.shape[3], z_ref.shape[4]

    def body(g, carry):
        rows = pl.ds(pl.multiple_of(g * BF16_SUBLANES, BF16_SUBLANES), BF16_SUBLANES)
        for l0 in range(0, tc, LANES):
            lanes = slice(l0, l0 + LANES)
            z = [(z_ref[0, n1, 0, rows, lanes].astype(F32), z_ref[0, n1, 1, rows, lanes].astype(F32))
                 for n1 in range(radix)]
            for k1, (p_r, p_i) in enumerate(_fft(z)):
                pr_ref[0, k1, rows, lanes] = p_r.astype(BF16)
                pi_ref[0, k1, rows, lanes] = p_i.astype(BF16)
        return carry

    lax.fori_loop(0, tq // BF16_SUBLANES, body, 0)


def _radix_combine(z, *, tq, tc):
    n_seq, radix, _, s2, d = z.shape
    out = jax.ShapeDtypeStruct((n_seq, radix, s2, d), BF16)
    oblk = pl.BlockSpec((1, radix, tq, tc), lambda b, q, c: (b, 0, q, c))
    pipelined = 2 * _nbytes((radix, 2, tq, tc), BF16)
    return pl.pallas_call(
        _radix_kernel,
        out_shape=(out, out),
        grid=(n_seq, s2 // tq, d // tc),
        in_specs=[pl.BlockSpec((1, radix, 2, tq, tc), lambda b, q, c: (b, 0, 0, q, c))],
        out_specs=(oblk, oblk),
        compiler_params=_params(("parallel", "parallel", "parallel"), pipelined),
        name="fourier_radix_combine",
    )(z)


def _fourier_out_kernel(scale, x_ref, pr_ref, pi_ref, d_ref, w_ref, o_ref, y_ref):
    @pl.when(pl.program_id(1) == 0)
    def _():
        gw = d_ref.shape[1]
        for q in range(pr_ref.shape[1] // gw):
            cols = slice(q * gw, (q + 1) * gw)
            lhs = jnp.concatenate([pr_ref[:, cols], pi_ref[:, cols]], axis=1)
            y_ref[:, cols] = (jnp.dot(lhs, d_ref[...], preferred_element_type=F32) * scale).astype(BF16)

    o_ref[...] = x_ref[...] + jnp.dot(y_ref[...], w_ref[...], preferred_element_type=F32)


def _fourier_out(x, p_r, p_i, dmat, w, scale, tok0, *, tm, tn):
    t, d = x.shape
    n_tok = p_r.shape[0]
    blk0 = tok0 // tm
    assert blk0 * tm == tok0 and n_tok % tm == 0
    pipelined = (2 * _nbytes((tm, tn), F32) + 2 * _nbytes((tm, d), BF16) + _nbytes(dmat.shape, BF16)
                 + _nbytes((d, tn), BF16))
    resident = _nbytes((tm, d), BF16) + 2 * _nbytes((tm, tn), F32)
    return pl.pallas_call(
        functools.partial(_fourier_out_kernel, scale),
        out_shape=jax.ShapeDtypeStruct((t, d), F32),
        grid=(n_tok // tm, d // tn),
        in_specs=[
            pl.BlockSpec((tm, tn), lambda i, j: (blk0 + i, j)),
            pl.BlockSpec((tm, d), lambda i, j: (i, 0)),
            pl.BlockSpec((tm, d), lambda i, j: (i, 0)),
            pl.BlockSpec(dmat.shape, lambda i, j: (0, 0)),
            pl.BlockSpec((d, tn), lambda i, j: (0, j)),
        ],
        out_specs=pl.BlockSpec((tm, tn), lambda i, j: (blk0 + i, j)),
        scratch_shapes=[pltpu.VMEM((tm, d), BF16)],
        input_output_aliases={0: 0},
        compiler_params=_params(("parallel", "arbitrary"), pipelined, resident),
        name="fourier_out_proj",
    )(x, p_r, p_i, dmat, w)


def _fourier_mix(x, gain, w_out, groups, *, tm, tn):
    d = x.shape[1]
    gw = d // FOURIER_GROUPS
    cc, sc = _dft_cos_sin(gw)
    dmat = jnp.concatenate([cc, sc], axis=0).astype(BF16)
    mixed = []
    for tok0, n_seq, s in groups:
        s2 = s // DFT_RADIX
        fc, fs = _dft_cos_sin(s2)
        n1 = jnp.arange(DFT_RADIX, dtype=jnp.int32)[:, None]
        k2 = jnp.arange(s2, dtype=jnp.int32)[None, :]
        tw = jnp.stack(_unit_circle(n1 * k2, s), axis=2)
        h = _deinterleave(x, gain, tok0, n_seq, s, tm=tm)
        z = _dft_dense(h, fc.astype(BF16), fs.astype(BF16), tw, tr=min(s2, 512), tc=_tile(d, 1024))
        p_r, p_i = _radix_combine(z, tq=min(s2, 128), tc=_tile(d, 256))
        mixed.append((p_r.reshape(n_seq * s, d), p_i.reshape(n_seq * s, d), float((s * gw) ** -0.5), tok0))
    for p_r, p_i, scale, tok0 in mixed:
        x = _fourier_out(x, p_r, p_i, dmat, w_out, scale, tok0, tm=tm, tn=tn)
    return x


def _trunk(parts, seqs, out_ranges, p):
    d = parts[0].shape[1]
    t = sum(a.shape[0] for a in parts)
    tm = _tile(math.gcd(*[n for _, n in seqs]), 1024)
    depth = p["ffn1_norm"].shape[0]
    a_width = p["ab_rpb"].shape[1] * HEAD_DIM
    b_width = d - a_width
    seq_starts = [s for s, _ in seqs]
    seq_ends = [s + n for s, n in seqs]
    seg_groups = [s // Q_TOK for s in seq_starts] + [seq_ends[-1] // Q_TOK]
    for s, n in seqs:
        assert s % tm == 0 and n % tm == 0 and n >= K_TOK
    groups = []
    for s, n in seqs:
        if groups and groups[-1][2] == n and groups[-1][0] + groups[-1][1] * n == s:
            groups[-1] = (groups[-1][0], groups[-1][1] + 1, n)
        else:
            groups.append((s, 1, n))
    bf = lambda a: a.astype(BF16)
    tf = _tile(p["ffn1_w_gate"].shape[2], 512)
    tn = _tile(d, 1024)
    x = None
    for i in range(depth):
        ffn1 = functools.partial(_ffn, gain=p["ffn1_norm"][i], w_gate=bf(p["ffn1_w_gate"][i]),
                                 w_up=bf(p["ffn1_w_up"][i]), w_down=bf(p["ffn1_w_down"][i]), tm=tm, tf=tf)
        if i == 0:
            tok0 = 0
            for part in parts:
                x = ffn1(part, out_tok0=tok0, out_total=t, out_buf=x)
                tok0 += part.shape[0]
        else:
            x = ffn1(x)
        j = i // 2
        if i % 2 == 0:
            w_in = bf(p["ab_w_in"][j])
            qkv = _qkv(x, p["mix_norm"][i], w_in, 3 * a_width, tm=tm, tn=_tile(a_width, 1024))
            c = _glu(x, p["mix_norm"][i], w_in, 3 * a_width, 3 * a_width + b_width, b_width, tm=tm, tn=_tile(b_width, 512))
            att = _attention(qkv, _attn_bias_rows(p["ab_rpb"][j]), seg_groups,
                             pairs_per_step=math.gcd(a_width // LANES, 4))
            cc = _conv_module(c, p["ab_conv_w"][j], p["ab_conv_b"][j], p["ab_ln_g"][j], p["ab_ln_b"][j],
                              seq_starts, seq_ends, tm=min(tm, 512))
            x = _outproj(x, att, cc, bf(p["ab_w_out"][j]), tm=tm, tn=tn)
        else:
            x = _fourier_mix(x, p["mix_norm"][i], bf(p["c_w_out"][j]), groups, tm=tm, tn=tn)
        ffn2 = functools.partial(_ffn, x, p["ffn2_norm"][i], bf(p["ffn2_w_gate"][i]), bf(p["ffn2_w_up"][i]),
                                 bf(p["ffn2_w_down"][i]), tm=tm, tf=tf)
        if i < depth - 1:
            x = ffn2()
    return [ffn2(out_gain=p["final_norm"], tok0=tok0, n_tok=n_tok) for tok0, n_tok in out_ranges]


def kernel(x_prompt, x_sample, ffn1_norm, ffn1_w_gate, ffn1_w_up, ffn1_w_down, mix_norm, ab_w_in, ab_rpb, ab_conv_w, ab_conv_b, ab_ln_g, ab_ln_b, ab_w_out, c_w_out, ffn2_norm, ffn2_w_gate, ffn2_w_up, ffn2_w_down, final_norm):
    params = dict(ffn1_norm=ffn1_norm, ffn1_w_gate=ffn1_w_gate, ffn1_w_up=ffn1_w_up, ffn1_w_down=ffn1_w_down,
                  mix_norm=mix_norm, ab_w_in=ab_w_in, ab_rpb=ab_rpb, ab_conv_w=ab_conv_w, ab_conv_b=ab_conv_b,
                  ab_ln_g=ab_ln_g, ab_ln_b=ab_ln_b, ab_w_out=ab_w_out, c_w_out=c_w_out, ffn2_norm=ffn2_norm,
                  ffn2_w_gate=ffn2_w_gate, ffn2_w_up=ffn2_w_up, ffn2_w_down=ffn2_w_down, final_norm=final_norm)
    bp, sp, d = x_prompt.shape
    bs, ss, _ = x_sample.shape
    n_s, n_p = bs * ss, bp * sp
    parts = [x_sample.reshape(n_s, d), x_prompt.reshape(n_p, d)]
    seqs = [(b * ss, ss) for b in range(bs)] + [(n_s + b * sp, sp) for b in range(bp)]
    y_sample, y_prompt = _trunk(parts, seqs, [(0, n_s), (n_s, n_p)], params)
    return (y_prompt.reshape(bp, sp, d), y_sample.reshape(bs, ss, d))
```

```python
import functools
import math

import numpy as np
import jax
import jax.numpy as jnp
from jax import lax
from jax.experimental import pallas as pl
from jax.experimental.pallas import tpu as pltpu

F32 = jnp.float32
BF16 = jnp.bfloat16

HEAD_DIM = 64
GRID_W = 64
WIN_ROWS = 8
WIN_COLS = 16
CONV_K = 31
FOURIER_GROUPS = 4
NORM_EPS = 1e-6

LANES = 128
F32_SUBLANES = 8
BF16_SUBLANES = 16
VMEM_BUDGET_BYTES = 60000 * 1024

Q_ROWS = 4
K_ROWS = 12
Q_TOK = Q_ROWS * GRID_W
K_TOK = K_ROWS * GRID_W
MASKED = -1e30
DFT_RADIX = 16
CONV_HALO = 16


def _vmem_limit(pipelined_bytes, resident_bytes=0):
    return int(min(VMEM_BUDGET_BYTES, 2 * pipelined_bytes + resident_bytes + (4 << 20)))


def _nbytes(shape, dtype):
    return math.prod(shape) * jnp.dtype(dtype).itemsize


def _params(semantics, pipelined_bytes, resident_bytes=0):
    return pltpu.CompilerParams(dimension_semantics=semantics,
                                vmem_limit_bytes=_vmem_limit(pipelined_bytes, resident_bytes))


def _rms(x, gain):
    return x * lax.rsqrt(jnp.mean(x * x, axis=-1, keepdims=True) + NORM_EPS) * gain


def _tile(n, preferred):
    best = LANES
    for c in range(LANES, min(n, preferred) + 1, LANES):
        if n % c == 0:
            best = c
    assert n % best == 0
    return best


def _seg_select(i, bounds, values):
    out = values[-1]
    for s in range(len(values) - 2, -1, -1):
        out = jnp.where(i < bounds[s + 1], values[s], out)
    return out


def _cast_kernel(w_ref, o_ref):
    o_ref[...] = w_ref[...].astype(BF16)


def _cast_bf16(w):
    n_layers, rows, cols = w.shape
    tr = _tile(rows, max(LANES, (4 << 20) // (4 * cols) // LANES * LANES))
    blk = (1, tr, cols)
    return pl.pallas_call(
        _cast_kernel,
        out_shape=jax.ShapeDtypeStruct(w.shape, BF16),
        grid=(n_layers, rows // tr),
        in_specs=[pl.BlockSpec(blk, lambda l, i: (l, i, 0))],
        out_specs=pl.BlockSpec(blk, lambda l, i: (l, i, 0)),
        compiler_params=_params(("parallel", "parallel"), _nbytes(blk, F32) + _nbytes(blk, BF16)),
        name="cast_bf16",
    )(w)


def _ffn_kernel(out_norm, x_ref, g_ref, wg_ref, wu_ref, wd_ref, og_ref, *rest):
    o_ref, h_ref = rest[-2:]
    j = pl.program_id(1)

    def mlp(h):
        gate = jnp.dot(h, wg_ref[...], preferred_element_type=F32)
        up = jnp.dot(h, wu_ref[...], preferred_element_type=F32)
        act = (gate * jax.nn.sigmoid(gate)) * (0.5 * up)
        return jnp.dot(act.astype(BF16), wd_ref[...], preferred_element_type=F32)

    @pl.when(j == 0)
    def _():
        x = x_ref[...]
        h = _rms(x, g_ref[...]).astype(BF16)
        h_ref[...] = h
        o_ref[...] = x + mlp(h)

    @pl.when(j != 0)
    def _():
        o_ref[...] += mlp(h_ref[...])

    if out_norm:
        @pl.when(j == pl.num_programs(1) - 1)
        def _():
            o_ref[...] = _rms(o_ref[...], og_ref[...])


def _wspec(layer, block, index_map):
    return pl.BlockSpec((None,) + block, lambda *grid_idx: (layer,) + tuple(index_map(*grid_idx)))


def _ffn(x, gain, w_gate, w_up, w_down, layer, *, tm, tf, out_gain=None, tok0=0, n_tok=None, out_tok0=0,
         out_total=None, out_buf=None):
    t, d = x.shape
    n_tok = t if n_tok is None else n_tok
    out_total = n_tok if out_total is None else out_total
    f = w_gate.shape[2]
    blk0, oblk0 = tok0 // tm, out_tok0 // tm
    assert blk0 * tm == tok0 and oblk0 * tm == out_tok0 and n_tok % tm == 0
    pipelined = (_nbytes((tm, d), F32) * 2 + 2 * _nbytes((d, tf), BF16) + _nbytes((tf, d), BF16))
    resident = _nbytes((tm, d), BF16) + 3 * _nbytes((tm, tf), F32)
    og = gain if out_gain is None else out_gain
    in_specs = [
        pl.BlockSpec((tm, d), lambda i, j: (blk0 + i, 0)),
        pl.BlockSpec((1, d), lambda i, j: (0, 0)),
        _wspec(layer, (d, tf), lambda i, j: (0, j)),
        _wspec(layer, (d, tf), lambda i, j: (0, j)),
        _wspec(layer, (tf, d), lambda i, j: (j, 0)),
        pl.BlockSpec((1, d), lambda i, j: (0, 0)),
    ]
    args = [x, gain.reshape(1, d), w_gate, w_up, w_down, og.reshape(1, d)]
    aliases = {}
    if out_buf is not None:
        assert out_buf.shape == (out_total, d)
        in_specs.append(pl.BlockSpec(memory_space=pl.ANY))
        args.append(out_buf)
        aliases = {len(args) - 1: 0}
    return pl.pallas_call(
        functools.partial(_ffn_kernel, out_gain is not None),
        out_shape=jax.ShapeDtypeStruct((out_total, d), F32),
        grid=(n_tok // tm, f // tf),
        in_specs=in_specs,
        out_specs=pl.BlockSpec((tm, d), lambda i, j: (oblk0 + i, 0)),
        scratch_shapes=[pltpu.VMEM((tm, d), BF16)],
        input_output_aliases=aliases,
        compiler_params=_params(("parallel", "arbitrary"), pipelined, resident),
        name="ffn",
    )(*args)


def _qkv_kernel(x_ref, g_ref, w_ref, o_ref, h_ref):
    def project(h):
        r = jnp.dot(h, w_ref[...], preferred_element_type=F32)
        for p in range(o_ref.shape[0]):
            o_ref[p] = r[:, p * LANES:(p + 1) * LANES].astype(BF16)

    @pl.when(pl.program_id(1) == 0)
    def _():
        h = _rms(x_ref[...], g_ref[...]).astype(BF16)
        h_ref[...] = h
        project(h)

    @pl.when(pl.program_id(1) != 0)
    def _():
        project(h_ref[...])


def _qkv(x, gain, w_in, layer, n_cols, *, tm, tn):
    t, d = x.shape
    pipelined = _nbytes((tm, d), F32) + _nbytes((d, tn), BF16) + _nbytes((tm, tn), BF16)
    resident = _nbytes((tm, d), BF16) + _nbytes((tm, tn), F32)
    return pl.pallas_call(
        _qkv_kernel,
        out_shape=jax.ShapeDtypeStruct((n_cols // LANES, t, LANES), BF16),
        grid=(t // tm, n_cols // tn),
        in_specs=[
            pl.BlockSpec((tm, d), lambda i, j: (i, 0)),
            pl.BlockSpec((1, d), lambda i, j: (0, 0)),
            _wspec(layer, (d, tn), lambda i, j: (0, j)),
        ],
        out_specs=pl.BlockSpec((tn // LANES, tm, LANES), lambda i, j: (j, i, 0)),
        scratch_shapes=[pltpu.VMEM((tm, d), BF16)],
        compiler_params=_params(("parallel", "arbitrary"), pipelined, resident),
        name="qkv_proj",
    )(x, gain.reshape(1, d), w_in)


def _glu_kernel(x_ref, g_ref, wa_ref, wg_ref, o_ref, h_ref):
    def project(h):
        a = jnp.dot(h, wa_ref[...], preferred_element_type=F32)
        g = jnp.dot(h, wg_ref[...], preferred_element_type=F32)
        o_ref[...] = a * jax.nn.sigmoid(g)

    @pl.when(pl.program_id(1) == 0)
    def _():
        h = _rms(x_ref[...], g_ref[...]).astype(BF16)
        h_ref[...] = h
        project(h)

    @pl.when(pl.program_id(1) != 0)
    def _():
        project(h_ref[...])


def _glu(x, gain, w_in, layer, a_col0, g_col0, width, *, tm, tn):
    t, d = x.shape
    pipelined = _nbytes((tm, d), F32) + 2 * _nbytes((d, tn), BF16) + _nbytes((tm, tn), F32)
    resident = _nbytes((tm, d), BF16) + 2 * _nbytes((tm, tn), F32)
    return pl.pallas_call(
        _glu_kernel,
        out_shape=jax.ShapeDtypeStruct((t, width), F32),
        grid=(t // tm, width // tn),
        in_specs=[
            pl.BlockSpec((tm, d), lambda i, j: (i, 0)),
            pl.BlockSpec((1, d), lambda i, j: (0, 0)),
            _wspec(layer, (d, tn), lambda i, j: (0, a_col0 // tn + j)),
            _wspec(layer, (d, tn), lambda i, j: (0, g_col0 // tn + j)),
        ],
        out_specs=pl.BlockSpec((tm, tn), lambda i, j: (i, j)),
        scratch_shapes=[pltpu.VMEM((tm, d), BF16)],
        compiler_params=_params(("parallel", "arbitrary"), pipelined, resident),
        name="glu_proj",
    )(x, gain.reshape(1, d), w_in, w_in)


N_ROW_OFFSETS = 2 * WIN_ROWS - 1
MASKED_ROW = N_ROW_OFFSETS


def _attn_bias_rows(rpb):
    n_heads = rpb.shape[0]
    pad = GRID_W - WIN_COLS
    padded = jnp.pad(rpb.astype(F32), ((0, 0), (0, 0), (pad, pad)))
    rows = jnp.stack([padded[:, :, GRID_W - 1 - c:2 * GRID_W - 1 - c] for c in range(GRID_W)], axis=2)
    c = np.arange(GRID_W)[:, None]
    kc = np.arange(GRID_W)[None, :]
    c0 = np.clip(c - WIN_COLS // 2, 0, GRID_W - WIN_COLS)
    c_ok = (kc >= c0) & (kc < c0 + WIN_COLS)
    rows = jnp.where(c_ok[None, None], rows, MASKED)
    rows = jnp.concatenate([rows, jnp.full((n_heads, 1, GRID_W, GRID_W), MASKED, F32)], axis=1)
    return jnp.concatenate([rows, rows], axis=3)


def _attn_kernel(lo_of, hi_of, q_ref, k0_ref, k1_ref, k2_ref, v0_ref, v1_ref, v2_ref, t_ref, o_ref, b_ref):
    i = pl.program_id(1)
    lo = lo_of(i)
    hi = hi_of(i)
    n_pairs = q_ref.shape[0]
    lane = lax.broadcasted_iota(jnp.int32, (GRID_W, LANES), 1)

    @pl.when((i == lo) | (i == lo + 1) | (i == hi - 1))
    def _():
        case = jnp.where(i == lo, 0, jnp.where(i == hi - 1, 2, 1))
        for qr in range(Q_ROWS):
            first = jnp.where(case == 0, 0, jnp.where(case == 1, qr, Q_ROWS))

            def entry(j):
                ok = (j >= first) & (j < first + WIN_ROWS)
                return jnp.where(ok, j - case * Q_ROWS - qr + (WIN_ROWS - 1), MASKED_ROW)

            for h in range(2 * n_pairs):
                r0 = (h % 2) * Q_TOK + qr * GRID_W
                for j in range(0, K_ROWS, 2):
                    left = t_ref[h, entry(j)]
                    right = t_ref[h, entry(j + 1)]
                    b_ref[h // 2, r0:r0 + GRID_W, j * GRID_W:(j + 2) * GRID_W] = jnp.where(lane < GRID_W, left, right)

    qlane = lax.broadcasted_iota(jnp.int32, (Q_TOK, LANES), 1)
    for p in range(n_pairs):
        q = q_ref[p]
        zero = jnp.zeros_like(q)
        q2 = jnp.concatenate([jnp.where(qlane < HEAD_DIM, q, zero), jnp.where(qlane >= HEAD_DIM, q, zero)], axis=0)
        k = jnp.concatenate([k0_ref[p], k1_ref[p], k2_ref[p]], axis=0)
        v = jnp.concatenate([v0_ref[p], v1_ref[p], v2_ref[p]], axis=0)
        s = lax.dot_general(q2, k, (((1,), (1,)), ((), ())), preferred_element_type=F32)
        s = s * (HEAD_DIM ** -0.5) + b_ref[p]
        e = jnp.exp(s - jnp.max(s, axis=-1, keepdims=True))
        denom = jnp.sum(e, axis=-1, keepdims=True)
        o2 = jnp.dot(e.astype(BF16), v, preferred_element_type=F32) / denom
        o_ref[p] = jnp.where(qlane < HEAD_DIM, o2[:Q_TOK], o2[Q_TOK:]).astype(BF16)


def _attention(qkv, bias_rows, seg_groups, *, pairs_per_step):
    n_pairs = qkv.shape[0] // 3
    t = qkv.shape[1]
    pb = pairs_per_step
    assert n_pairs % pb == 0
    lo_of = lambda i: _seg_select(i, seg_groups, seg_groups[:-1])
    hi_of = lambda i: _seg_select(i, seg_groups, seg_groups[1:])
    window = lambda i: jnp.clip(i - 1, lo_of(i), hi_of(i) - K_ROWS // Q_ROWS)

    blk = (pb, Q_TOK, LANES)
    kv_specs = [pl.BlockSpec(blk, functools.partial(lambda p, i, off, j: (off + p, window(i) + j, 0), off=off, j=j))
                for off in (n_pairs // pb, 2 * n_pairs // pb) for j in range(K_ROWS // Q_ROWS)]
    tab_blk = (2 * pb,) + bias_rows.shape[1:]
    pipelined = 8 * _nbytes(blk, BF16) + _nbytes(tab_blk, F32)
    resident = _nbytes((pb, 2 * Q_TOK, K_TOK), F32) + 3 * pb * _nbytes((2 * Q_TOK, K_TOK), F32)
    return pl.pallas_call(
        functools.partial(_attn_kernel, lo_of, hi_of),
        out_shape=jax.ShapeDtypeStruct((n_pairs, t, LANES), BF16),
        grid=(n_pairs // pb, t // Q_TOK),
        in_specs=[pl.BlockSpec(blk, lambda p, i: (p, i, 0))] + kv_specs + [
            pl.BlockSpec(tab_blk, lambda p, i: (p, 0, 0, 0))],
        out_specs=pl.BlockSpec(blk, lambda p, i: (p, i, 0)),
        scratch_shapes=[pltpu.VMEM((pb, 2 * Q_TOK, K_TOK), F32)],
        compiler_params=_params(("arbitrary", "arbitrary"), pipelined, resident),
        name="nbr_attention",
    )(qkv, *([qkv] * 6), bias_rows)


def _conv_kernel(first_tiles, last_tiles, prev_ref, cur_ref, next_ref, w_ref, cb_ref, lg_ref, lb_ref,
                 o_ref, buf_ref, sh_ref, y_ref):
    i = pl.program_id(0)
    tm, c = cur_ref.shape
    is_first = functools.reduce(jnp.logical_or, [i == s for s in first_tiles])
    is_last = functools.reduce(jnp.logical_or, [i == s for s in last_tiles])
    buf_ref[0:CONV_HALO, :] = jnp.where(is_first, 0.0, prev_ref[...])
    buf_ref[CONV_HALO:CONV_HALO + tm, :] = cur_ref[...]
    buf_ref[CONV_HALO + tm:, :] = jnp.where(is_last, 0.0, next_ref[...])
    chunk = 64
    base = CONV_HALO - CONV_K // 2
    n_sh = sh_ref.shape[1]
    for l0 in range(0, c, LANES):
        lanes = slice(l0, l0 + LANES)
        for o in range(1, F32_SUBLANES):
            sh_ref[o] = buf_ref[o:o + n_sh, lanes]
        def chunk_body(ci, carry, lanes=lanes):
            t0 = pl.multiple_of(ci * chunk, chunk)
            acc = jnp.broadcast_to(cb_ref[:, lanes], (chunk, LANES))
            for k in range(CONV_K):
                o = (base + k) % F32_SUBLANES
                rows = pl.ds(t0 + (base + k - o), chunk)
                src = buf_ref[rows, lanes] if o == 0 else sh_ref[o, rows, :]
                acc = acc + w_ref[k:k + 1, lanes] * src
            y_ref[pl.ds(t0, chunk), lanes] = acc
            return carry

        lax.fori_loop(0, tm // chunk, chunk_body, 0)
    y = y_ref[...]
    mu = jnp.mean(y, axis=-1, keepdims=True)
    yc = y - mu
    var = jnp.mean(yc * yc, axis=-1, keepdims=True)
    z = yc * lax.rsqrt(var + NORM_EPS) * lg_ref[...] + lb_ref[...]
    o_ref[...] = (z * jax.nn.sigmoid(z)).astype(BF16)


def _conv_module(c, conv_w, conv_b, ln_g, ln_b, seq_starts, seq_ends, *, tm):
    t, width = c.shape
    first_tiles = tuple(s // tm for s in seq_starts)
    last_tiles = tuple(e // tm - 1 for e in seq_ends)
    hb = tm // CONV_HALO
    n_hb = t // CONV_HALO
    pipelined = _nbytes((tm + 2 * CONV_HALO, width), F32) + _nbytes((tm, width), BF16)
    resident = _nbytes((2 * tm + 2 * CONV_HALO, width), F32) + 3 * _nbytes((tm, width), F32)
    row = lambda a: a.reshape(1, width)
    return pl.pallas_call(
        functools.partial(_conv_kernel, first_tiles, last_tiles),
        out_shape=jax.ShapeDtypeStruct((t, width), BF16),
        grid=(t // tm,),
        in_specs=[
            pl.BlockSpec((CONV_HALO, width), lambda i: (jnp.maximum(i * hb - 1, 0), 0)),
            pl.BlockSpec((tm, width), lambda i: (i, 0)),
            pl.BlockSpec((CONV_HALO, width), lambda i: (jnp.minimum((i + 1) * hb, n_hb - 1), 0)),
            pl.BlockSpec((CONV_K, width), lambda i: (0, 0)),
            pl.BlockSpec((1, width), lambda i: (0, 0)),
            pl.BlockSpec((1, width), lambda i: (0, 0)),
            pl.BlockSpec((1, width), lambda i: (0, 0)),
        ],
        out_specs=pl.BlockSpec((tm, width), lambda i: (i, 0)),
        scratch_shapes=[pltpu.VMEM((tm + 2 * CONV_HALO, width), F32),
                        pltpu.VMEM((F32_SUBLANES, tm + 2 * CONV_HALO - F32_SUBLANES, LANES), F32),
                        pltpu.VMEM((tm, width), F32)],
        compiler_params=_params(("parallel",), pipelined, resident),
        name="conv_module",
    )(c, c, c, conv_w, row(conv_b), row(ln_g), row(ln_b))


def _outproj_kernel(x_ref, att_ref, cc_ref, w1_ref, w2_ref, o_ref):
    att = jnp.concatenate([att_ref[p] for p in range(att_ref.shape[0])], axis=1)
    acc = jnp.dot(att, w1_ref[...], preferred_element_type=F32)
    acc = acc + jnp.dot(cc_ref[...], w2_ref[...], preferred_element_type=F32)
    o_ref[...] = x_ref[...] + acc


def _outproj(x, att, cc, w_out, layer, *, tm, tn):
    t, d = x.shape
    n_pairs = att.shape[0]
    wa = n_pairs * LANES
    wb = cc.shape[1]
    assert wa == wb and wa + wb == w_out.shape[1]
    pipelined = (2 * _nbytes((tm, tn), F32) + _nbytes((tm, wa), BF16) + _nbytes((tm, wb), BF16)
                 + _nbytes((wa + wb, tn), BF16))
    resident = 2 * _nbytes((tm, tn), F32) + _nbytes((tm, wa), BF16)
    return pl.pallas_call(
        _outproj_kernel,
        out_shape=jax.ShapeDtypeStruct((t, d), F32),
        grid=(t // tm, d // tn),
        in_specs=[
            pl.BlockSpec((tm, tn), lambda i, j: (i, j)),
            pl.BlockSpec((n_pairs, tm, LANES), lambda i, j: (0, i, 0)),
            pl.BlockSpec((tm, wb), lambda i, j: (i, 0)),
            _wspec(layer, (wa, tn), lambda i, j: (0, j)),
            _wspec(layer, (wb, tn), lambda i, j: (1, j)),
        ],
        out_specs=pl.BlockSpec((tm, tn), lambda i, j: (i, j)),
        compiler_params=_params(("parallel", "arbitrary"), pipelined, resident),
        name="attn_conv_out_proj",
    )(x, att, cc, w_out, w_out)


def _unit_circle(num, den):
    ang = (num % den).astype(F32) * F32(2.0 * np.pi / den)
    return jnp.cos(ang), jnp.sin(ang)


def _dft_cos_sin(n):
    idx = jnp.arange(n, dtype=jnp.int32)
    return _unit_circle(idx[:, None] * idx[None, :], n)


def _deinterleave_kernel(x_ref, g_ref, o_ref, hs_ref):
    tm, d = x_ref.shape
    radix = o_ref.shape[1]
    h = _rms(x_ref[...], g_ref[...])
    for s in range(d // LANES):
        hs_ref[s] = h[:, s * LANES:(s + 1) * LANES]
    for n1 in range(radix):
        for s in range(d // LANES):
            o_ref[0, n1, :, s * LANES:(s + 1) * LANES] = hs_ref[s, pl.ds(n1, tm // radix, stride=radix), :].astype(BF16)


def _deinterleave(x, gain, tok0, n_seq, s, *, tm):
    d = x.shape[1]
    tiles = s // tm
    blk0 = tok0 // tm
    assert blk0 * tm == tok0 and tiles * tm == s and (tm // DFT_RADIX) % BF16_SUBLANES == 0
    pipelined = _nbytes((tm, d), F32) + _nbytes((tm, d), BF16)
    resident = 2 * _nbytes((tm, d), F32)
    return pl.pallas_call(
        _deinterleave_kernel,
        out_shape=jax.ShapeDtypeStruct((n_seq, DFT_RADIX, s // DFT_RADIX, d), BF16),
        grid=(n_seq, tiles),
        in_specs=[pl.BlockSpec((tm, d), lambda b, i: (blk0 + b * tiles + i, 0)),
                  pl.BlockSpec((1, d), lambda b, i: (0, 0))],
        out_specs=pl.BlockSpec((1, DFT_RADIX, tm // DFT_RADIX, d), lambda b, i: (b, 0, i, 0)),
        scratch_shapes=[pltpu.VMEM((d // LANES, tm, LANES), F32)],
        compiler_params=_params(("parallel", "parallel"), pipelined, resident),
        name="fourier_deinterleave",
    )(x, gain.reshape(1, d))


def _dft_dense_kernel(fc_ref, fs_ref, tw_ref, h_ref, o_ref):
    h = h_ref[0, 0]
    yc = jnp.dot(fc_ref[...], h, preferred_element_type=F32)
    ys = jnp.dot(fs_ref[...], h, preferred_element_type=F32)
    tc = tw_ref[0, :, 0:1]
    ts = tw_ref[0, :, 1:2]
    o_ref[0, 0, 0] = (tc * yc - ts * ys).astype(BF16)
    o_ref[0, 0, 1] = (-(tc * ys + ts * yc)).astype(BF16)


def _dft_dense(h, fc, fs, tw, *, tr, tc):
    n_seq, radix, s2, d = h.shape
    pipelined = 2 * _nbytes((tr, s2), BF16) + _nbytes((s2, tc), BF16) + _nbytes((2, tr, tc), BF16)
    resident = 4 * _nbytes((tr, tc), F32)
    return pl.pallas_call(
        _dft_dense_kernel,
        out_shape=jax.ShapeDtypeStruct((n_seq, radix, 2, s2, d), BF16),
        grid=(n_seq, radix, d // tc, s2 // tr),
        in_specs=[
            pl.BlockSpec((tr, s2), lambda b, n, c, r: (r, 0)),
            pl.BlockSpec((tr, s2), lambda b, n, c, r: (r, 0)),
            pl.BlockSpec((1, tr, 2), lambda b, n, c, r: (n, r, 0)),
            pl.BlockSpec((1, 1, s2, tc), lambda b, n, c, r: (b, n, 0, c)),
        ],
        out_specs=pl.BlockSpec((1, 1, 2, tr, tc), lambda b, n, c, r: (b, n, 0, r, c)),
        compiler_params=_params(("parallel", "parallel", "parallel", "arbitrary"), pipelined, resident),
        name="fourier_dense_dft",
    )(fc, fs, tw, h)


def _fft(z):
    n = len(z)
    if n == 1:
        return z
    even, odd = _fft(z[0::2]), _fft(z[1::2])
    out = [None] * n
    for k in range(n // 2):
        o_r, o_i = odd[k]
        if k == 0:
            t_r, t_i = o_r, o_i
        elif 4 * k == n:
            t_r, t_i = o_i, -o_r
        else:
            c, s = math.cos(2 * math.pi * k / n), math.sin(2 * math.pi * k / n)
            t_r, t_i = c * o_r + s * o_i, c * o_i - s * o_r
        e_r, e_i = even[k]
        out[k] = (e_r + t_r, e_i + t_i)
        out[k + n // 2] = (e_r - t_r, e_i - t_i)
    return out


def _radix_kernel(z_ref, pr_ref, pi_ref):
    radix, tq, tc = z_ref.shape[1], z_ref.shape[3], z_ref.shape[4]

    def body(g, carry):
        rows = pl.ds(pl.multiple_of(g * BF16_SUBLANES, BF16_SUBLANES), BF16_SUBLANES)
        for l0 in range(0, tc, LANES):
            lanes = slice(l0, l0 + LANES)
            z = [(z_ref[0, n1, 0, rows, lanes].astype(F32), z_ref[0, n1, 1, rows, lanes].astype(F32))
                 for n1 in range(radix)]
            for k1, (p_r, p_i) in enumerate(_fft(z)):
                pr_ref[0, k1, rows, lanes] = p_r.astype(BF16)
                pi_ref[0, k1, rows, lanes] = p_i.astype(BF16)
        return carry

    lax.fori_loop(0, tq // BF16_SUBLANES, body, 0)


def _radix_combine(z, *, tq, tc):
    n_seq, radix, _, s2, d = z.shape
    out = jax.ShapeDtypeStruct((n_seq, radix, s2, d), BF16)
    oblk = pl.BlockSpec((1, radix, tq, tc), lambda b, q, c: (b, 0, q, c))
    pipelined = 2 * _nbytes((radix, 2, tq, tc), BF16)
    return pl.pallas_call(
        _radix_kernel,
        out_shape=(out, out),
        grid=(n_seq, s2 // tq, d // tc),
        in_specs=[pl.BlockSpec((1, radix, 2, tq, tc), lambda b, q, c: (b, 0, 0, q, c))],
        out_specs=(oblk, oblk),
        compiler_params=_params(("parallel", "parallel", "parallel"), pipelined),
        name="fourier_radix_combine",
    )(z)


def _fourier_out_kernel(scale, x_ref, pr_ref, pi_ref, d_ref, w_ref, o_ref, y_ref):
    @pl.when(pl.program_id(1) == 0)
    def _():
        gw = d_ref.shape[1]
        for q in range(pr_ref.shape[1] // gw):
            cols = slice(q * gw, (q + 1) * gw)
            lhs = jnp.concatenate([pr_ref[:, cols], pi_ref[:, cols]], axis=1)
            y_ref[:, cols] = (jnp.dot(lhs, d_ref[...], preferred_element_type=F32) * scale).astype(BF16)

    o_ref[...] = x_ref[...] + jnp.dot(y_ref[...], w_ref[...], preferred_element_type=F32)


def _fourier_out(x, p_r, p_i, dmat, w, layer, scale, tok0, *, tm, tn):
    t, d = x.shape
    n_tok = p_r.shape[0]
    blk0 = tok0 // tm
    assert blk0 * tm == tok0 and n_tok % tm == 0
    pipelined = (2 * _nbytes((tm, tn), F32) + 2 * _nbytes((tm, d), BF16) + _nbytes(dmat.shape, BF16)
                 + _nbytes((d, tn), BF16))
    resident = _nbytes((tm, d), BF16) + 2 * _nbytes((tm, tn), F32)
    return pl.pallas_call(
        functools.partial(_fourier_out_kernel, scale),
        out_shape=jax.ShapeDtypeStruct((t, d), F32),
        grid=(n_tok // tm, d // tn),
        in_specs=[
            pl.BlockSpec((tm, tn), lambda i, j: (blk0 + i, j)),
            pl.BlockSpec((tm, d), lambda i, j: (i, 0)),
            pl.BlockSpec((tm, d), lambda i, j: (i, 0)),
            pl.BlockSpec(dmat.shape, lambda i, j: (0, 0)),
            _wspec(layer, (d, tn), lambda i, j: (0, j)),
        ],
        out_specs=pl.BlockSpec((tm, tn), lambda i, j: (blk0 + i, j)),
        scratch_shapes=[pltpu.VMEM((tm, d), BF16)],
        input_output_aliases={0: 0},
        compiler_params=_params(("parallel", "arbitrary"), pipelined, resident),
        name="fourier_out_proj",
    )(x, p_r, p_i, dmat, w)


def _fourier_mix(x, gain, w_out, layer, groups, *, tm, tn):
    d = x.shape[1]
    gw = d // FOURIER_GROUPS
    cc, sc = _dft_cos_sin(gw)
    dmat = jnp.concatenate([cc, sc], axis=0).astype(BF16)
    mixed = []
    for tok0, n_seq, s in groups:
        s2 = s // DFT_RADIX
        fc, fs = _dft_cos_sin(s2)
        n1 = jnp.arange(DFT_RADIX, dtype=jnp.int32)[:, None]
        k2 = jnp.arange(s2, dtype=jnp.int32)[None, :]
        tw = jnp.stack(_unit_circle(n1 * k2, s), axis=2)
        h = _deinterleave(x, gain, tok0, n_seq, s, tm=tm)
        z = _dft_dense(h, fc.astype(BF16), fs.astype(BF16), tw, tr=min(s2, 512), tc=_tile(d, 1024))
        p_r, p_i = _radix_combine(z, tq=min(s2, 128), tc=_tile(d, 256))
        mixed.append((p_r.reshape(n_seq * s, d), p_i.reshape(n_seq * s, d), float((s * gw) ** -0.5), tok0))
    for p_r, p_i, scale, tok0 in mixed:
        x = _fourier_out(x, p_r, p_i, dmat, w_out, layer, scale, tok0, tm=tm, tn=tn)
    return x


def _trunk(parts, seqs, out_ranges, p):
    d = parts[0].shape[1]
    t = sum(a.shape[0] for a in parts)
    tm = _tile(math.gcd(*[n for _, n in seqs]), 1024)
    depth = p["ffn1_norm"].shape[0]
    a_width = p["ab_rpb"].shape[1] * HEAD_DIM
    b_width = d - a_width
    seq_starts = [s for s, _ in seqs]
    seq_ends = [s + n for s, n in seqs]
    seg_groups = [s // Q_TOK for s in seq_starts] + [seq_ends[-1] // Q_TOK]
    for s, n in seqs:
        assert s % tm == 0 and n % tm == 0 and n >= K_TOK
    groups = []
    for s, n in seqs:
        if groups and groups[-1][2] == n and groups[-1][0] + groups[-1][1] * n == s:
            groups[-1] = (groups[-1][0], groups[-1][1] + 1, n)
        else:
            groups.append((s, 1, n))
    w = {name: _cast_bf16(p[name]) for name in (
        "ffn1_w_gate", "ffn1_w_up", "ffn1_w_down", "ffn2_w_gate", "ffn2_w_up", "ffn2_w_down",
        "ab_w_in", "ab_w_out", "c_w_out")}
    tf = _tile(p["ffn1_w_gate"].shape[2], 512)
    tn = _tile(d, 1024)
    x = None
    for i in range(depth):
        ffn1 = functools.partial(_ffn, gain=p["ffn1_norm"][i], w_gate=w["ffn1_w_gate"], w_up=w["ffn1_w_up"],
                                 w_down=w["ffn1_w_down"], layer=i, tm=tm, tf=tf)
        if i == 0:
            tok0 = 0
            for part in parts:
                x = ffn1(part, out_tok0=tok0, out_total=t, out_buf=x)
                tok0 += part.shape[0]
        else:
            x = ffn1(x)
        j = i // 2
        if i % 2 == 0:
            qkv = _qkv(x, p["mix_norm"][i], w["ab_w_in"], j, 3 * a_width, tm=tm, tn=_tile(a_width, 1024))
            c = _glu(x, p["mix_norm"][i], w["ab_w_in"], j, 3 * a_width, 3 * a_width + b_width, b_width,
                     tm=tm, tn=_tile(b_width, 512))
            att = _attention(qkv, _attn_bias_rows(p["ab_rpb"][j]), seg_groups,
                             pairs_per_step=math.gcd(a_width // LANES, 8))
            cc = _conv_module(c, p["ab_conv_w"][j], p["ab_conv_b"][j], p["ab_ln_g"][j], p["ab_ln_b"][j],
                              seq_starts, seq_ends, tm=min(tm, 512))
            x = _outproj(x, att, cc, w["ab_w_out"], j, tm=tm, tn=tn)
        else:
            x = _fourier_mix(x, p["mix_norm"][i], w["c_w_out"], j, groups, tm=tm, tn=tn)
        ffn2 = functools.partial(_ffn, x, p["ffn2_norm"][i], w["ffn2_w_gate"], w["ffn2_w_up"], w["ffn2_w_down"], i,
                                 tm=tm, tf=tf)
        if i < depth - 1:
            x = ffn2()
    return [ffn2(out_gain=p["final_norm"], tok0=tok0, n_tok=n_tok) for tok0, n_tok in out_ranges]


def kernel(x_prompt, x_sample, ffn1_norm, ffn1_w_gate, ffn1_w_up, ffn1_w_down, mix_norm, ab_w_in, ab_rpb, ab_conv_w, ab_conv_b, ab_ln_g, ab_ln_b, ab_w_out, c_w_out, ffn2_norm, ffn2_w_gate, ffn2_w_up, ffn2_w_down, final_norm):
    params = dict(ffn1_norm=ffn1_norm, ffn1_w_gate=ffn1_w_gate, ffn1_w_up=ffn1_w_up, ffn1_w_down=ffn1_w_down,
                  mix_norm=mix_norm, ab_w_in=ab_w_in, ab_rpb=ab_rpb, ab_conv_w=ab_conv_w, ab_conv_b=ab_conv_b,
                  ab_ln_g=ab_ln_g, ab_ln_b=ab_ln_b, ab_w_out=ab_w_out, c_w_out=c_w_out, ffn2_norm=ffn2_norm,
                  ffn2_w_gate=ffn2_w_gate, ffn2_w_up=ffn2_w_up, ffn2_w_down=ffn2_w_down, final_norm=final_norm)
    bp, sp, d = x_prompt.shape
    bs, ss, _ = x_sample.shape
    n_s, n_p = bs * ss, bp * sp
    parts = [x_sample.reshape(n_s, d), x_prompt.reshape(n_p, d)]
    seqs = [(b * ss, ss) for b in range(bs)] + [(n_s + b * sp, sp) for b in range(bp)]
    y_sample, y_prompt = _trunk(parts, seqs, [(0, n_s), (n_s, n_p)], params)
    return (y_prompt.reshape(bp, sp, d), y_sample.reshape(bs, ss, d))
```

```python
import functools
import math

import numpy as np
import jax
import jax.numpy as jnp
from jax import lax
from jax.experimental import pallas as pl
from jax.experimental.pallas import tpu as pltpu

F32 = jnp.float32
BF16 = jnp.bfloat16

HEAD_DIM = 64
GRID_W = 64
WIN_ROWS = 8
WIN_COLS = 16
CONV_K = 31
FOURIER_GROUPS = 4
NORM_EPS = 1e-6

LANES = 128
F32_SUBLANES = 8
BF16_SUBLANES = 16
VMEM_BUDGET_BYTES = 60000 * 1024

Q_ROWS = 4
K_ROWS = 12
Q_TOK = Q_ROWS * GRID_W
K_TOK = K_ROWS * GRID_W
MASKED = -1e30
DFT_RADIX = 16
CONV_HALO = 16


def _vmem_limit(pipelined_bytes, resident_bytes=0):
    return int(min(VMEM_BUDGET_BYTES, 2 * pipelined_bytes + resident_bytes + (4 << 20)))


def _nbytes(shape, dtype):
    return math.prod(shape) * jnp.dtype(dtype).itemsize


def _params(semantics, pipelined_bytes, resident_bytes=0):
    return pltpu.CompilerParams(dimension_semantics=semantics,
                                vmem_limit_bytes=_vmem_limit(pipelined_bytes, resident_bytes))


def _rms(x, gain):
    return x * lax.rsqrt(jnp.mean(x * x, axis=-1, keepdims=True) + NORM_EPS) * gain


def _tile(n, preferred):
    best = LANES
    for c in range(LANES, min(n, preferred) + 1, LANES):
        if n % c == 0:
            best = c
    assert n % best == 0
    return best


def _seg_select(i, bounds, values):
    out = values[-1]
    for s in range(len(values) - 2, -1, -1):
        out = jnp.where(i < bounds[s + 1], values[s], out)
    return out


def _cast_kernel(w_ref, o_ref):
    o_ref[...] = w_ref[...].astype(BF16)


def _cast_bf16(w):
    n_layers, rows, cols = w.shape
    tr = _tile(rows, max(LANES, (4 << 20) // (4 * cols) // LANES * LANES))
    blk = (1, tr, cols)
    return pl.pallas_call(
        _cast_kernel,
        out_shape=jax.ShapeDtypeStruct(w.shape, BF16),
        grid=(n_layers, rows // tr),
        in_specs=[pl.BlockSpec(blk, lambda l, i: (l, i, 0))],
        out_specs=pl.BlockSpec(blk, lambda l, i: (l, i, 0)),
        compiler_params=_params(("parallel", "parallel"), _nbytes(blk, F32) + _nbytes(blk, BF16)),
        name="cast_bf16",
    )(w)


def _ffn_kernel(out_norm, x_ref, g_ref, wg_ref, wu_ref, wd_ref, og_ref, *rest):
    o_ref, h_ref = rest[-2:]
    j = pl.program_id(1)

    def mlp(h):
        gate = jnp.dot(h, wg_ref[...], preferred_element_type=F32)
        up = jnp.dot(h, wu_ref[...], preferred_element_type=F32)
        act = (gate * jax.nn.sigmoid(gate)) * (0.5 * up)
        return jnp.dot(act.astype(BF16), wd_ref[...], preferred_element_type=F32)

    @pl.when(j == 0)
    def _():
        x = x_ref[...]
        h = _rms(x, g_ref[...]).astype(BF16)
        h_ref[...] = h
        o_ref[...] = x + mlp(h)

    @pl.when(j != 0)
    def _():
        o_ref[...] += mlp(h_ref[...])

    if out_norm:
        @pl.when(j == pl.num_programs(1) - 1)
        def _():
            o_ref[...] = _rms(o_ref[...], og_ref[...])


def _ffn_vmem(tm, tf, d):
    pipelined = 2 * _nbytes((tm, d), F32) + 3 * _nbytes((d, tf), BF16)
    resident = _nbytes((tm, d), BF16) + _nbytes((tm, tf), F32)
    return pipelined, resident


def _ffn_hidden_tile(tm, d, f):
    best = None
    for tf in range(LANES, f + 1, LANES):
        pipelined, resident = _ffn_vmem(tm, tf, d)
        if f % tf == 0 and 2 * pipelined + resident + (4 << 20) <= VMEM_BUDGET_BYTES:
            best = tf
    assert best is not None
    return best


def _wspec(layer, block, index_map):
    return pl.BlockSpec((None,) + block, lambda *grid_idx: (layer,) + tuple(index_map(*grid_idx)))


def _ffn(x, gain, w_gate, w_up, w_down, layer, *, tm, tf, out_gain=None, tok0=0, n_tok=None, out_tok0=0,
         out_total=None, out_buf=None):
    t, d = x.shape
    n_tok = t if n_tok is None else n_tok
    out_total = n_tok if out_total is None else out_total
    f = w_gate.shape[2]
    blk0, oblk0 = tok0 // tm, out_tok0 // tm
    assert blk0 * tm == tok0 and oblk0 * tm == out_tok0 and n_tok % tm == 0
    pipelined, resident = _ffn_vmem(tm, tf, d)
    og = gain if out_gain is None else out_gain
    in_specs = [
        pl.BlockSpec((tm, d), lambda i, j: (blk0 + i, 0)),
        pl.BlockSpec((1, d), lambda i, j: (0, 0)),
        _wspec(layer, (d, tf), lambda i, j: (0, j)),
        _wspec(layer, (d, tf), lambda i, j: (0, j)),
        _wspec(layer, (tf, d), lambda i, j: (j, 0)),
        pl.BlockSpec((1, d), lambda i, j: (0, 0)),
    ]
    args = [x, gain.reshape(1, d), w_gate, w_up, w_down, og.reshape(1, d)]
    aliases = {}
    if out_buf is not None:
        assert out_buf.shape == (out_total, d)
        in_specs.append(pl.BlockSpec(memory_space=pl.ANY))
        args.append(out_buf)
        aliases = {len(args) - 1: 0}
    return pl.pallas_call(
        functools.partial(_ffn_kernel, out_gain is not None),
        out_shape=jax.ShapeDtypeStruct((out_total, d), F32),
        grid=(n_tok // tm, f // tf),
        in_specs=in_specs,
        out_specs=pl.BlockSpec((tm, d), lambda i, j: (oblk0 + i, 0)),
        scratch_shapes=[pltpu.VMEM((tm, d), BF16)],
        input_output_aliases=aliases,
        compiler_params=_params(("parallel", "arbitrary"), pipelined, resident),
        name="ffn",
    )(*args)


def _qkv_kernel(x_ref, g_ref, w_ref, o_ref, h_ref):
    def project(h):
        r = jnp.dot(h, w_ref[...], preferred_element_type=F32)
        for p in range(o_ref.shape[0]):
            o_ref[p] = r[:, p * LANES:(p + 1) * LANES].astype(BF16)

    @pl.when(pl.program_id(1) == 0)
    def _():
        h = _rms(x_ref[...], g_ref[...]).astype(BF16)
        h_ref[...] = h
        project(h)

    @pl.when(pl.program_id(1) != 0)
    def _():
        project(h_ref[...])


def _qkv(x, gain, w_in, layer, n_cols, *, tm, tn):
    t, d = x.shape
    pipelined = _nbytes((tm, d), F32) + _nbytes((d, tn), BF16) + _nbytes((tm, tn), BF16)
    resident = _nbytes((tm, d), BF16) + _nbytes((tm, tn), F32)
    return pl.pallas_call(
        _qkv_kernel,
        out_shape=jax.ShapeDtypeStruct((n_cols // LANES, t, LANES), BF16),
        grid=(t // tm, n_cols // tn),
        in_specs=[
            pl.BlockSpec((tm, d), lambda i, j: (i, 0)),
            pl.BlockSpec((1, d), lambda i, j: (0, 0)),
            _wspec(layer, (d, tn), lambda i, j: (0, j)),
        ],
        out_specs=pl.BlockSpec((tn // LANES, tm, LANES), lambda i, j: (j, i, 0)),
        scratch_shapes=[pltpu.VMEM((tm, d), BF16)],
        compiler_params=_params(("parallel", "arbitrary"), pipelined, resident),
        name="qkv_proj",
    )(x, gain.reshape(1, d), w_in)


def _glu_kernel(x_ref, g_ref, wa_ref, wg_ref, o_ref, h_ref):
    def project(h):
        a = jnp.dot(h, wa_ref[...], preferred_element_type=F32)
        g = jnp.dot(h, wg_ref[...], preferred_element_type=F32)
        o_ref[...] = a * jax.nn.sigmoid(g)

    @pl.when(pl.program_id(1) == 0)
    def _():
        h = _rms(x_ref[...], g_ref[...]).astype(BF16)
        h_ref[...] = h
        project(h)

    @pl.when(pl.program_id(1) != 0)
    def _():
        project(h_ref[...])


def _glu(x, gain, w_in, layer, a_col0, g_col0, width, *, tm, tn):
    t, d = x.shape
    pipelined = _nbytes((tm, d), F32) + 2 * _nbytes((d, tn), BF16) + _nbytes((tm, tn), F32)
    resident = _nbytes((tm, d), BF16) + 2 * _nbytes((tm, tn), F32)
    return pl.pallas_call(
        _glu_kernel,
        out_shape=jax.ShapeDtypeStruct((t, width), F32),
        grid=(t // tm, width // tn),
        in_specs=[
            pl.BlockSpec((tm, d), lambda i, j: (i, 0)),
            pl.BlockSpec((1, d), lambda i, j: (0, 0)),
            _wspec(layer, (d, tn), lambda i, j: (0, a_col0 // tn + j)),
            _wspec(layer, (d, tn), lambda i, j: (0, g_col0 // tn + j)),
        ],
        out_specs=pl.BlockSpec((tm, tn), lambda i, j: (i, j)),
        scratch_shapes=[pltpu.VMEM((tm, d), BF16)],
        compiler_params=_params(("parallel", "arbitrary"), pipelined, resident),
        name="glu_proj",
    )(x, gain.reshape(1, d), w_in, w_in)


N_ROW_OFFSETS = 2 * WIN_ROWS - 1
MASKED_ROW = N_ROW_OFFSETS


def _attn_bias_rows(rpb):
    n_heads = rpb.shape[0]
    pad = GRID_W - WIN_COLS
    padded = jnp.pad(rpb.astype(F32), ((0, 0), (0, 0), (pad, pad)))
    rows = jnp.stack([padded[:, :, GRID_W - 1 - c:2 * GRID_W - 1 - c] for c in range(GRID_W)], axis=2)
    c = np.arange(GRID_W)[:, None]
    kc = np.arange(GRID_W)[None, :]
    c0 = np.clip(c - WIN_COLS // 2, 0, GRID_W - WIN_COLS)
    c_ok = (kc >= c0) & (kc < c0 + WIN_COLS)
    rows = jnp.where(c_ok[None, None], rows, MASKED)
    rows = jnp.concatenate([rows, jnp.full((n_heads, 1, GRID_W, GRID_W), MASKED, F32)], axis=1)
    return jnp.concatenate([rows, rows], axis=3)


def _attn_kernel(lo_of, hi_of, q_ref, k0_ref, k1_ref, k2_ref, v0_ref, v1_ref, v2_ref, t_ref, o_ref, b_ref):
    i = pl.program_id(1)
    lo = lo_of(i)
    hi = hi_of(i)
    n_pairs = q_ref.shape[0]
    lane = lax.broadcasted_iota(jnp.int32, (GRID_W, LANES), 1)

    @pl.when((i == lo) | (i == lo + 1) | (i == hi - 1))
    def _():
        case = jnp.where(i == lo, 0, jnp.where(i == hi - 1, 2, 1))
        for qr in range(Q_ROWS):
            first = jnp.where(case == 0, 0, jnp.where(case == 1, qr, Q_ROWS))

            def entry(j):
                ok = (j >= first) & (j < first + WIN_ROWS)
                return jnp.where(ok, j - case * Q_ROWS - qr + (WIN_ROWS - 1), MASKED_ROW)

            for h in range(2 * n_pairs):
                r0 = (h % 2) * Q_TOK + qr * GRID_W
                for j in range(0, K_ROWS, 2):
                    left = t_ref[h, entry(j)]
                    right = t_ref[h, entry(j + 1)]
                    b_ref[h // 2, r0:r0 + GRID_W, j * GRID_W:(j + 2) * GRID_W] = jnp.where(lane < GRID_W, left, right)

    qlane = lax.broadcasted_iota(jnp.int32, (Q_TOK, LANES), 1)
    for p in range(n_pairs):
        q = q_ref[p]
        zero = jnp.zeros_like(q)
        q2 = jnp.concatenate([jnp.where(qlane < HEAD_DIM, q, zero), jnp.where(qlane >= HEAD_DIM, q, zero)], axis=0)
        k = jnp.concatenate([k0_ref[p], k1_ref[p], k2_ref[p]], axis=0)
        v = jnp.concatenate([v0_ref[p], v1_ref[p], v2_ref[p]], axis=0)
        s = lax.dot_general(q2, k, (((1,), (1,)), ((), ())), preferred_element_type=F32)
        s = s * (HEAD_DIM ** -0.5) + b_ref[p]
        e = jnp.exp(s - jnp.max(s, axis=-1, keepdims=True))
        denom = jnp.sum(e, axis=-1, keepdims=True)
        o2 = jnp.dot(e.astype(BF16), v, preferred_element_type=F32) / denom
        o_ref[p] = jnp.where(qlane < HEAD_DIM, o2[:Q_TOK], o2[Q_TOK:]).astype(BF16)


def _attention(qkv, bias_rows, seg_groups, *, pairs_per_step):
    n_pairs = qkv.shape[0] // 3
    t = qkv.shape[1]
    pb = pairs_per_step
    assert n_pairs % pb == 0
    lo_of = lambda i: _seg_select(i, seg_groups, seg_groups[:-1])
    hi_of = lambda i: _seg_select(i, seg_groups, seg_groups[1:])
    window = lambda i: jnp.clip(i - 1, lo_of(i), hi_of(i) - K_ROWS // Q_ROWS)

    blk = (pb, Q_TOK, LANES)
    kv_specs = [pl.BlockSpec(blk, functools.partial(lambda p, i, off, j: (off + p, window(i) + j, 0), off=off, j=j))
                for off in (n_pairs // pb, 2 * n_pairs // pb) for j in range(K_ROWS // Q_ROWS)]
    tab_blk = (2 * pb,) + bias_rows.shape[1:]
    pipelined = 8 * _nbytes(blk, BF16) + _nbytes(tab_blk, F32)
    resident = _nbytes((pb, 2 * Q_TOK, K_TOK), F32) + 3 * pb * _nbytes((2 * Q_TOK, K_TOK), F32)
    return pl.pallas_call(
        functools.partial(_attn_kernel, lo_of, hi_of),
        out_shape=jax.ShapeDtypeStruct((n_pairs, t, LANES), BF16),
        grid=(n_pairs // pb, t // Q_TOK),
        in_specs=[pl.BlockSpec(blk, lambda p, i: (p, i, 0))] + kv_specs + [
            pl.BlockSpec(tab_blk, lambda p, i: (p, 0, 0, 0))],
        out_specs=pl.BlockSpec(blk, lambda p, i: (p, i, 0)),
        scratch_shapes=[pltpu.VMEM((pb, 2 * Q_TOK, K_TOK), F32)],
        compiler_params=_params(("arbitrary", "arbitrary"), pipelined, resident),
        name="nbr_attention",
    )(qkv, *([qkv] * 6), bias_rows)


def _conv_kernel(first_tiles, last_tiles, prev_ref, cur_ref, next_ref, w_ref, cb_ref, lg_ref, lb_ref,
                 o_ref, buf_ref, sh_ref, y_ref):
    i = pl.program_id(0)
    tm, c = cur_ref.shape
    is_first = functools.reduce(jnp.logical_or, [i == s for s in first_tiles])
    is_last = functools.reduce(jnp.logical_or, [i == s for s in last_tiles])
    buf_ref[0:CONV_HALO, :] = jnp.where(is_first, 0.0, prev_ref[...])
    buf_ref[CONV_HALO:CONV_HALO + tm, :] = cur_ref[...]
    buf_ref[CONV_HALO + tm:, :] = jnp.where(is_last, 0.0, next_ref[...])
    chunk = 128
    base = CONV_HALO - CONV_K // 2
    n_sh = sh_ref.shape[1]
    for l0 in range(0, c, LANES):
        lanes = slice(l0, l0 + LANES)
        for o in range(1, F32_SUBLANES):
            sh_ref[o] = buf_ref[o:o + n_sh, lanes]
        def chunk_body(ci, carry, lanes=lanes):
            t0 = pl.multiple_of(ci * chunk, chunk)
            acc = jnp.broadcast_to(cb_ref[:, lanes], (chunk, LANES))
            for k in range(CONV_K):
                o = (base + k) % F32_SUBLANES
                rows = pl.ds(t0 + (base + k - o), chunk)
                src = buf_ref[rows, lanes] if o == 0 else sh_ref[o, rows, :]
                acc = acc + w_ref[k:k + 1, lanes] * src
            y_ref[pl.ds(t0, chunk), lanes] = acc
            return carry

        lax.fori_loop(0, tm // chunk, chunk_body, 0)
    y = y_ref[...]
    mu = jnp.mean(y, axis=-1, keepdims=True)
    yc = y - mu
    var = jnp.mean(yc * yc, axis=-1, keepdims=True)
    z = yc * lax.rsqrt(var + NORM_EPS) * lg_ref[...] + lb_ref[...]
    o_ref[...] = (z * jax.nn.sigmoid(z)).astype(BF16)


def _conv_module(c, conv_w, conv_b, ln_g, ln_b, seq_starts, seq_ends, *, tm):
    t, width = c.shape
    first_tiles = tuple(s // tm for s in seq_starts)
    last_tiles = tuple(e // tm - 1 for e in seq_ends)
    hb = tm // CONV_HALO
    n_hb = t // CONV_HALO
    pipelined = _nbytes((tm + 2 * CONV_HALO, width), F32) + _nbytes((tm, width), BF16)
    resident = _nbytes((2 * tm + 2 * CONV_HALO, width), F32) + 3 * _nbytes((tm, width), F32)
    row = lambda a: a.reshape(1, width)
    return pl.pallas_call(
        functools.partial(_conv_kernel, first_tiles, last_tiles),
        out_shape=jax.ShapeDtypeStruct((t, width), BF16),
        grid=(t // tm,),
        in_specs=[
            pl.BlockSpec((CONV_HALO, width), lambda i: (jnp.maximum(i * hb - 1, 0), 0)),
            pl.BlockSpec((tm, width), lambda i: (i, 0)),
            pl.BlockSpec((CONV_HALO, width), lambda i: (jnp.minimum((i + 1) * hb, n_hb - 1), 0)),
            pl.BlockSpec((CONV_K, width), lambda i: (0, 0)),
            pl.BlockSpec((1, width), lambda i: (0, 0)),
            pl.BlockSpec((1, width), lambda i: (0, 0)),
            pl.BlockSpec((1, width), lambda i: (0, 0)),
        ],
        out_specs=pl.BlockSpec((tm, width), lambda i: (i, 0)),
        scratch_shapes=[pltpu.VMEM((tm + 2 * CONV_HALO, width), F32),
                        pltpu.VMEM((F32_SUBLANES, tm + 2 * CONV_HALO - F32_SUBLANES, LANES), F32),
                        pltpu.VMEM((tm, width), F32)],
        compiler_params=_params(("parallel",), pipelined, resident),
        name="conv_module",
    )(c, c, c, conv_w, row(conv_b), row(ln_g), row(ln_b))


def _outproj_kernel(x_ref, att_ref, cc_ref, w1_ref, w2_ref, o_ref):
    att = jnp.concatenate([att_ref[p] for p in range(att_ref.shape[0])], axis=1)
    acc = jnp.dot(att, w1_ref[...], preferred_element_type=F32)
    acc = acc + jnp.dot(cc_ref[...], w2_ref[...], preferred_element_type=F32)
    o_ref[...] = x_ref[...] + acc


def _outproj(x, att, cc, w_out, layer, *, tm, tn):
    t, d = x.shape
    n_pairs = att.shape[0]
    wa = n_pairs * LANES
    wb = cc.shape[1]
    assert wa == wb and wa + wb == w_out.shape[1]
    pipelined = (2 * _nbytes((tm, tn), F32) + _nbytes((tm, wa), BF16) + _nbytes((tm, wb), BF16)
                 + _nbytes((wa + wb, tn), BF16))
    resident = 2 * _nbytes((tm, tn), F32) + _nbytes((tm, wa), BF16)
    return pl.pallas_call(
        _outproj_kernel,
        out_shape=jax.ShapeDtypeStruct((t, d), F32),
        grid=(t // tm, d // tn),
        in_specs=[
            pl.BlockSpec((tm, tn), lambda i, j: (i, j)),
            pl.BlockSpec((n_pairs, tm, LANES), lambda i, j: (0, i, 0)),
            pl.BlockSpec((tm, wb), lambda i, j: (i, 0)),
            _wspec(layer, (wa, tn), lambda i, j: (0, j)),
            _wspec(layer, (wb, tn), lambda i, j: (1, j)),
        ],
        out_specs=pl.BlockSpec((tm, tn), lambda i, j: (i, j)),
        compiler_params=_params(("parallel", "arbitrary"), pipelined, resident),
        name="attn_conv_out_proj",
    )(x, att, cc, w_out, w_out)


def _unit_circle(num, den):
    ang = (num % den).astype(F32) * F32(2.0 * np.pi / den)
    return jnp.cos(ang), jnp.sin(ang)


def _dft_cos_sin(n):
    idx = jnp.arange(n, dtype=jnp.int32)
    return _unit_circle(idx[:, None] * idx[None, :], n)


def _deinterleave_kernel(x_ref, g_ref, o_ref, hs_ref):
    tm, d = x_ref.shape
    radix = o_ref.shape[1]
    h = _rms(x_ref[...], g_ref[...])
    for s in range(d // LANES):
        hs_ref[s] = h[:, s * LANES:(s + 1) * LANES]
    for n1 in range(radix):
        for s in range(d // LANES):
            o_ref[0, n1, :, s * LANES:(s + 1) * LANES] = hs_ref[s, pl.ds(n1, tm // radix, stride=radix), :].astype(BF16)


def _deinterleave(x, gain, tok0, n_seq, s, *, tm):
    d = x.shape[1]
    tiles = s // tm
    blk0 = tok0 // tm
    assert blk0 * tm == tok0 and tiles * tm == s and (tm // DFT_RADIX) % BF16_SUBLANES == 0
    pipelined = _nbytes((tm, d), F32) + _nbytes((tm, d), BF16)
    resident = 2 * _nbytes((tm, d), F32)
    return pl.pallas_call(
        _deinterleave_kernel,
        out_shape=jax.ShapeDtypeStruct((n_seq, DFT_RADIX, s // DFT_RADIX, d), BF16),
        grid=(n_seq, tiles),
        in_specs=[pl.BlockSpec((tm, d), lambda b, i: (blk0 + b * tiles + i, 0)),
                  pl.BlockSpec((1, d), lambda b, i: (0, 0))],
        out_specs=pl.BlockSpec((1, DFT_RADIX, tm // DFT_RADIX, d), lambda b, i: (b, 0, i, 0)),
        scratch_shapes=[pltpu.VMEM((d // LANES, tm, LANES), F32)],
        compiler_params=_params(("parallel", "parallel"), pipelined, resident),
        name="fourier_deinterleave",
    )(x, gain.reshape(1, d))


def _dft_dense_kernel(fc_ref, fs_ref, tw_ref, h_ref, o_ref):
    h = h_ref[0, 0]
    yc = jnp.dot(fc_ref[...], h, preferred_element_type=F32)
    ys = jnp.dot(fs_ref[...], h, preferred_element_type=F32)
    tc = tw_ref[0, :, 0:1]
    ts = tw_ref[0, :, 1:2]
    o_ref[0, 0, 0] = (tc * yc - ts * ys).astype(BF16)
    o_ref[0, 0, 1] = (-(tc * ys + ts * yc)).astype(BF16)


def _dft_dense(h, fc, fs, tw, *, tr, tc):
    n_seq, radix, s2, d = h.shape
    pipelined = 2 * _nbytes((tr, s2), BF16) + _nbytes((s2, tc), BF16) + _nbytes((2, tr, tc), BF16)
    resident = 4 * _nbytes((tr, tc), F32)
    return pl.pallas_call(
        _dft_dense_kernel,
        out_shape=jax.ShapeDtypeStruct((n_seq, radix, 2, s2, d), BF16),
        grid=(n_seq, radix, d // tc, s2 // tr),
        in_specs=[
            pl.BlockSpec((tr, s2), lambda b, n, c, r: (r, 0)),
            pl.BlockSpec((tr, s2), lambda b, n, c, r: (r, 0)),
            pl.BlockSpec((1, tr, 2), lambda b, n, c, r: (n, r, 0)),
            pl.BlockSpec((1, 1, s2, tc), lambda b, n, c, r: (b, n, 0, c)),
        ],
        out_specs=pl.BlockSpec((1, 1, 2, tr, tc), lambda b, n, c, r: (b, n, 0, r, c)),
        compiler_params=_params(("parallel", "parallel", "parallel", "arbitrary"), pipelined, resident),
        name="fourier_dense_dft",
    )(fc, fs, tw, h)


def _fft(z):
    n = len(z)
    if n == 1:
        return z
    even, odd = _fft(z[0::2]), _fft(z[1::2])
    out = [None] * n
    for k in range(n // 2):
        o_r, o_i = odd[k]
        if k == 0:
            t_r, t_i = o_r, o_i
        elif 4 * k == n:
            t_r, t_i = o_i, -o_r
        else:
            c, s = math.cos(2 * math.pi * k / n), math.sin(2 * math.pi * k / n)
            t_r, t_i = c * o_r + s * o_i, c * o_i - s * o_r
        e_r, e_i = even[k]
        out[k] = (e_r + t_r, e_i + t_i)
        out[k + n // 2] = (e_r - t_r, e_i - t_i)
    return out


def _radix_kernel(z_ref, pr_ref, pi_ref):
    radix, tq, tc = z_ref.shape[1], z_ref.shape[3], z_ref.shape[4]

    def body(g, carry):
        rows = pl.ds(pl.multiple_of(g * BF16_SUBLANES, BF16_SUBLANES), BF16_SUBLANES)
        for l0 in range(0, tc, LANES):
            lanes = slice(l0, l0 + LANES)
            z = [(z_ref[0, n1, 0, rows, lanes].astype(F32), z_ref[0, n1, 1, rows, lanes].astype(F32))
                 for n1 in range(radix)]
            for k1, (p_r, p_i) in enumerate(_fft(z)):
                pr_ref[0, k1, rows, lanes] = p_r.astype(BF16)
                pi_ref[0, k1, rows, lanes] = p_i.astype(BF16)
        return carry

    lax.fori_loop(0, tq // BF16_SUBLANES, body, 0)


def _radix_combine(z, *, tq, tc):
    n_seq, radix, _, s2, d = z.shape
    out = jax.ShapeDtypeStruct((n_seq, radix, s2, d), BF16)
    oblk = pl.BlockSpec((1, radix, tq, tc), lambda b, q, c: (b, 0, q, c))
    pipelined = 2 * _nbytes((radix, 2, tq, tc), BF16)
    return pl.pallas_call(
        _radix_kernel,
        out_shape=(out, out),
        grid=(n_seq, s2 // tq, d // tc),
        in_specs=[pl.BlockSpec((1, radix, 2, tq, tc), lambda b, q, c: (b, 0, 0, q, c))],
        out_specs=(oblk, oblk),
        compiler_params=_params(("parallel", "parallel", "parallel"), pipelined),
        name="fourier_radix_combine",
    )(z)


def _fourier_out_kernel(scale, x_ref, pr_ref, pi_ref, d_ref, w_ref, o_ref, y_ref):
    @pl.when(pl.program_id(1) == 0)
    def _():
        gw = d_ref.shape[1]
        for q in range(pr_ref.shape[1] // gw):
            cols = slice(q * gw, (q + 1) * gw)
            lhs = jnp.concatenate([pr_ref[:, cols], pi_ref[:, cols]], axis=1)
            y_ref[:, cols] = (jnp.dot(lhs, d_ref[...], preferred_element_type=F32) * scale).astype(BF16)

    o_ref[...] = x_ref[...] + jnp.dot(y_ref[...], w_ref[...], preferred_element_type=F32)


def _fourier_out(x, p_r, p_i, dmat, w, layer, scale, tok0, *, tm, tn):
    t, d = x.shape
    n_tok = p_r.shape[0]
    blk0 = tok0 // tm
    assert blk0 * tm == tok0 and n_tok % tm == 0
    pipelined = (2 * _nbytes((tm, tn), F32) + 2 * _nbytes((tm, d), BF16) + _nbytes(dmat.shape, BF16)
                 + _nbytes((d, tn), BF16))
    resident = _nbytes((tm, d), BF16) + 2 * _nbytes((tm, tn), F32)
    return pl.pallas_call(
        functools.partial(_fourier_out_kernel, scale),
        out_shape=jax.ShapeDtypeStruct((t, d), F32),
        grid=(n_tok // tm, d // tn),
        in_specs=[
            pl.BlockSpec((tm, tn), lambda i, j: (blk0 + i, j)),
            pl.BlockSpec((tm, d), lambda i, j: (i, 0)),
            pl.BlockSpec((tm, d), lambda i, j: (i, 0)),
            pl.BlockSpec(dmat.shape, lambda i, j: (0, 0)),
            _wspec(layer, (d, tn), lambda i, j: (0, j)),
        ],
        out_specs=pl.BlockSpec((tm, tn), lambda i, j: (blk0 + i, j)),
        scratch_shapes=[pltpu.VMEM((tm, d), BF16)],
        input_output_aliases={0: 0},
        compiler_params=_params(("parallel", "arbitrary"), pipelined, resident),
        name="fourier_out_proj",
    )(x, p_r, p_i, dmat, w)


def _fourier_mix(x, gain, w_out, layer, groups, *, tm, tn):
    d = x.shape[1]
    gw = d // FOURIER_GROUPS
    cc, sc = _dft_cos_sin(gw)
    dmat = jnp.concatenate([cc, sc], axis=0).astype(BF16)
    mixed = []
    for tok0, n_seq, s in groups:
        s2 = s // DFT_RADIX
        fc, fs = _dft_cos_sin(s2)
        n1 = jnp.arange(DFT_RADIX, dtype=jnp.int32)[:, None]
        k2 = jnp.arange(s2, dtype=jnp.int32)[None, :]
        tw = jnp.stack(_unit_circle(n1 * k2, s), axis=2)
        h = _deinterleave(x, gain, tok0, n_seq, s, tm=tm)
        z = _dft_dense(h, fc.astype(BF16), fs.astype(BF16), tw, tr=min(s2, 1024), tc=_tile(d, 1024))
        p_r, p_i = _radix_combine(z, tq=min(s2, 128), tc=_tile(d, 256))
        mixed.append((p_r.reshape(n_seq * s, d), p_i.reshape(n_seq * s, d), float((s * gw) ** -0.5), tok0))
    for p_r, p_i, scale, tok0 in mixed:
        x = _fourier_out(x, p_r, p_i, dmat, w_out, layer, scale, tok0, tm=tm // 2, tn=d)
    return x


def _trunk(parts, seqs, out_ranges, p):
    d = parts[0].shape[1]
    t = sum(a.shape[0] for a in parts)
    tm = _tile(math.gcd(*[n for _, n in seqs]), 1024)
    depth = p["ffn1_norm"].shape[0]
    a_width = p["ab_rpb"].shape[1] * HEAD_DIM
    b_width = d - a_width
    seq_starts = [s for s, _ in seqs]
    seq_ends = [s + n for s, n in seqs]
    seg_groups = [s // Q_TOK for s in seq_starts] + [seq_ends[-1] // Q_TOK]
    for s, n in seqs:
        assert s % tm == 0 and n % tm == 0 and n >= K_TOK
    groups = []
    for s, n in seqs:
        if groups and groups[-1][2] == n and groups[-1][0] + groups[-1][1] * n == s:
            groups[-1] = (groups[-1][0], groups[-1][1] + 1, n)
        else:
            groups.append((s, 1, n))
    w = {name: _cast_bf16(p[name]) for name in (
        "ffn1_w_gate", "ffn1_w_up", "ffn1_w_down", "ffn2_w_gate", "ffn2_w_up", "ffn2_w_down",
        "ab_w_in", "ab_w_out", "c_w_out")}
    tm_ffn, tf = tm, _ffn_hidden_tile(tm, d, p["ffn1_w_gate"].shape[2])
    tn = _tile(d, 1024)
    x = None
    for i in range(depth):
        ffn1 = functools.partial(_ffn, gain=p["ffn1_norm"][i], w_gate=w["ffn1_w_gate"], w_up=w["ffn1_w_up"],
                                 w_down=w["ffn1_w_down"], layer=i, tm=tm_ffn, tf=tf)
        if i == 0:
            tok0 = 0
            for part in parts:
                x = ffn1(part, out_tok0=tok0, out_total=t, out_buf=x)
                tok0 += part.shape[0]
        else:
            x = ffn1(x)
        j = i // 2
        if i % 2 == 0:
            qkv = _qkv(x, p["mix_norm"][i], w["ab_w_in"], j, 3 * a_width, tm=tm, tn=_tile(a_width, 1024))
            c = _glu(x, p["mix_norm"][i], w["ab_w_in"], j, 3 * a_width, 3 * a_width + b_width, b_width,
                     tm=tm, tn=_tile(b_width, 512))
            att = _attention(qkv, _attn_bias_rows(p["ab_rpb"][j]), seg_groups,
                             pairs_per_step=math.gcd(a_width // LANES, 8))
            cc = _conv_module(c, p["ab_conv_w"][j], p["ab_conv_b"][j], p["ab_ln_g"][j], p["ab_ln_b"][j],
                              seq_starts, seq_ends, tm=min(tm, 512))
            x = _outproj(x, att, cc, w["ab_w_out"], j, tm=tm, tn=tn)
        else:
            x = _fourier_mix(x, p["mix_norm"][i], w["c_w_out"], j, groups, tm=tm, tn=tn)
        ffn2 = functools.partial(_ffn, x, p["ffn2_norm"][i], w["ffn2_w_gate"], w["ffn2_w_up"], w["ffn2_w_down"], i,
                                 tm=tm_ffn, tf=tf)
        if i < depth - 1:
            x = ffn2()
    return [ffn2(out_gain=p["final_norm"], tok0=tok0, n_tok=n_tok) for tok0, n_tok in out_ranges]


def kernel(x_prompt, x_sample, ffn1_norm, ffn1_w_gate, ffn1_w_up, ffn1_w_down, mix_norm, ab_w_in, ab_rpb, ab_conv_w, ab_conv_b, ab_ln_g, ab_ln_b, ab_w_out, c_w_out, ffn2_norm, ffn2_w_gate, ffn2_w_up, ffn2_w_down, final_norm):
    params = dict(ffn1_norm=ffn1_norm, ffn1_w_gate=ffn1_w_gate, ffn1_w_up=ffn1_w_up, ffn1_w_down=ffn1_w_down,
                  mix_norm=mix_norm, ab_w_in=ab_w_in, ab_rpb=ab_rpb, ab_conv_w=ab_conv_w, ab_conv_b=ab_conv_b,
                  ab_ln_g=ab_ln_g, ab_ln_b=ab_ln_b, ab_w_out=ab_w_out, c_w_out=c_w_out, ffn2_norm=ffn2_norm,
                  ffn2_w_gate=ffn2_w_gate, ffn2_w_up=ffn2_w_up, ffn2_w_down=ffn2_w_down, final_norm=final_norm)
    bp, sp, d = x_prompt.shape
    bs, ss, _ = x_sample.shape
    n_s, n_p = bs * ss, bp * sp
    parts = [x_sample.reshape(n_s, d), x_prompt.reshape(n_p, d)]
    seqs = [(b * ss, ss) for b in range(bs)] + [(n_s + b * sp, sp) for b in range(bp)]
    y_sample, y_prompt = _trunk(parts, seqs, [(0, n_s), (n_s, n_p)], params)
    return (y_prompt.reshape(bp, sp, d), y_sample.reshape(bs, ss, d))
```

```python
import functools
import math

import numpy as np
import jax
import jax.numpy as jnp
from jax import lax
from jax.experimental import pallas as pl
from jax.experimental.pallas import tpu as pltpu

F32 = jnp.float32
BF16 = jnp.bfloat16

HEAD_DIM = 64
GRID_W = 64
WIN_ROWS = 8
WIN_COLS = 16
CONV_K = 31
FOURIER_GROUPS = 4
NORM_EPS = 1e-6

LANES = 128
F32_SUBLANES = 8
BF16_SUBLANES = 16
VMEM_BUDGET_BYTES = 60000 * 1024

Q_ROWS = 4
K_ROWS = 12
Q_TOK = Q_ROWS * GRID_W
K_TOK = K_ROWS * GRID_W
MASKED = -1e30
DFT_RADIX = 16
CONV_HALO = 16
CAST_TILE_BYTES = 12 << 20


def _vmem_limit(pipelined_bytes, resident_bytes=0):
    return int(min(VMEM_BUDGET_BYTES, 2 * pipelined_bytes + resident_bytes + (4 << 20)))


def _nbytes(shape, dtype):
    return math.prod(shape) * jnp.dtype(dtype).itemsize


def _params(semantics, pipelined_bytes, resident_bytes=0):
    return pltpu.CompilerParams(dimension_semantics=semantics,
                                vmem_limit_bytes=_vmem_limit(pipelined_bytes, resident_bytes))


def _rms(x, gain):
    return x * lax.rsqrt(jnp.mean(x * x, axis=-1, keepdims=True) + NORM_EPS) * gain


def _tile(n, preferred):
    best = LANES
    for c in range(LANES, min(n, preferred) + 1, LANES):
        if n % c == 0:
            best = c
    assert n % best == 0
    return best


def _seg_select(i, bounds, values):
    out = values[-1]
    for s in range(len(values) - 2, -1, -1):
        out = jnp.where(i < bounds[s + 1], values[s], out)
    return out


def _cast_kernel(w_ref, o_ref):
    o_ref[...] = w_ref[...].astype(BF16)


def _cast_bf16(w):
    n_layers, rows, cols = w.shape
    tr = _tile(rows, max(LANES, CAST_TILE_BYTES // (4 * cols) // LANES * LANES))
    blk = (1, tr, cols)
    return pl.pallas_call(
        _cast_kernel,
        out_shape=jax.ShapeDtypeStruct(w.shape, BF16),
        grid=(n_layers, rows // tr),
        in_specs=[pl.BlockSpec(blk, lambda l, i: (l, i, 0))],
        out_specs=pl.BlockSpec(blk, lambda l, i: (l, i, 0)),
        compiler_params=_params(("parallel", "parallel"), _nbytes(blk, F32) + _nbytes(blk, BF16)),
        name="cast_bf16",
    )(w)


def _ffn_kernel(out_norm, x_ref, g_ref, wg_ref, wu_ref, wd_ref, og_ref, *rest):
    o_ref, h_ref = rest[-2:]
    j = pl.program_id(1)

    def mlp(h):
        gate = jnp.dot(h, wg_ref[...], preferred_element_type=F32)
        up = jnp.dot(h, wu_ref[...], preferred_element_type=F32)
        act = (gate * jax.nn.sigmoid(gate)) * (0.5 * up)
        return jnp.dot(act.astype(BF16), wd_ref[...], preferred_element_type=F32)

    @pl.when(j == 0)
    def _():
        x = x_ref[...]
        h = _rms(x, g_ref[...]).astype(BF16)
        h_ref[...] = h
        o_ref[...] = x + mlp(h)

    @pl.when(j != 0)
    def _():
        o_ref[...] += mlp(h_ref[...])

    if out_norm:
        @pl.when(j == pl.num_programs(1) - 1)
        def _():
            o_ref[...] = _rms(o_ref[...], og_ref[...])


def _ffn_vmem(tm, tf, d):
    pipelined = 2 * _nbytes((tm, d), F32) + 3 * _nbytes((d, tf), BF16)
    resident = _nbytes((tm, d), BF16) + _nbytes((tm, tf), F32)
    return pipelined, resident


def _ffn_hidden_tile(tm, d, f):
    best = None
    for tf in range(LANES, f + 1, LANES):
        pipelined, resident = _ffn_vmem(tm, tf, d)
        if f % tf == 0 and 2 * pipelined + resident + (4 << 20) <= VMEM_BUDGET_BYTES:
            best = tf
    assert best is not None
    return best


def _wspec(layer, block, index_map):
    return pl.BlockSpec((None,) + block, lambda *grid_idx: (layer,) + tuple(index_map(*grid_idx)))


def _ffn(x, gain, w_gate, w_up, w_down, layer, *, tm, tf, out_gain=None, tok0=0, n_tok=None, out_tok0=0,
         out_total=None, out_buf=None):
    t, d = x.shape
    n_tok = t if n_tok is None else n_tok
    out_total = n_tok if out_total is None else out_total
    f = w_gate.shape[2]
    blk0, oblk0 = tok0 // tm, out_tok0 // tm
    assert blk0 * tm == tok0 and oblk0 * tm == out_tok0 and n_tok % tm == 0
    pipelined, resident = _ffn_vmem(tm, tf, d)
    og = gain if out_gain is None else out_gain
    in_specs = [
        pl.BlockSpec((tm, d), lambda i, j: (blk0 + i, 0)),
        pl.BlockSpec((1, d), lambda i, j: (0, 0)),
        _wspec(layer, (d, tf), lambda i, j: (0, j)),
        _wspec(layer, (d, tf), lambda i, j: (0, j)),
        _wspec(layer, (tf, d), lambda i, j: (j, 0)),
        pl.BlockSpec((1, d), lambda i, j: (0, 0)),
    ]
    args = [x, gain.reshape(1, d), w_gate, w_up, w_down, og.reshape(1, d)]
    aliases = {}
    if out_buf is not None:
        assert out_buf.shape == (out_total, d)
        in_specs.append(pl.BlockSpec(memory_space=pl.ANY))
        args.append(out_buf)
        aliases = {len(args) - 1: 0}
    return pl.pallas_call(
        functools.partial(_ffn_kernel, out_gain is not None),
        out_shape=jax.ShapeDtypeStruct((out_total, d), F32),
        grid=(n_tok // tm, f // tf),
        in_specs=in_specs,
        out_specs=pl.BlockSpec((tm, d), lambda i, j: (oblk0 + i, 0)),
        scratch_shapes=[pltpu.VMEM((tm, d), BF16)],
        input_output_aliases=aliases,
        compiler_params=_params(("parallel", "arbitrary"), pipelined, resident),
        name="ffn",
    )(*args)


def _qkv_kernel(x_ref, g_ref, w_ref, o_ref, h_ref):
    def project(h):
        r = jnp.dot(h, w_ref[...], preferred_element_type=F32)
        for p in range(o_ref.shape[0]):
            o_ref[p] = r[:, p * LANES:(p + 1) * LANES].astype(BF16)

    @pl.when(pl.program_id(1) == 0)
    def _():
        h = _rms(x_ref[...], g_ref[...]).astype(BF16)
        h_ref[...] = h
        project(h)

    @pl.when(pl.program_id(1) != 0)
    def _():
        project(h_ref[...])


def _qkv(x, gain, w_in, layer, n_cols, *, tm, tn):
    t, d = x.shape
    pipelined = _nbytes((tm, d), F32) + _nbytes((d, tn), BF16) + _nbytes((tm, tn), BF16)
    resident = _nbytes((tm, d), BF16) + _nbytes((tm, tn), F32)
    return pl.pallas_call(
        _qkv_kernel,
        out_shape=jax.ShapeDtypeStruct((n_cols // LANES, t, LANES), BF16),
        grid=(t // tm, n_cols // tn),
        in_specs=[
            pl.BlockSpec((tm, d), lambda i, j: (i, 0)),
            pl.BlockSpec((1, d), lambda i, j: (0, 0)),
            _wspec(layer, (d, tn), lambda i, j: (0, j)),
        ],
        out_specs=pl.BlockSpec((tn // LANES, tm, LANES), lambda i, j: (j, i, 0)),
        scratch_shapes=[pltpu.VMEM((tm, d), BF16)],
        compiler_params=_params(("parallel", "arbitrary"), pipelined, resident),
        name="qkv_proj",
    )(x, gain.reshape(1, d), w_in)


def _glu_kernel(x_ref, g_ref, wa_ref, wg_ref, o_ref, h_ref):
    def project(h):
        a = jnp.dot(h, wa_ref[...], preferred_element_type=F32)
        g = jnp.dot(h, wg_ref[...], preferred_element_type=F32)
        o_ref[...] = a * jax.nn.sigmoid(g)

    @pl.when(pl.program_id(1) == 0)
    def _():
        h = _rms(x_ref[...], g_ref[...]).astype(BF16)
        h_ref[...] = h
        project(h)

    @pl.when(pl.program_id(1) != 0)
    def _():
        project(h_ref[...])


def _glu(x, gain, w_in, layer, a_col0, g_col0, width, *, tm, tn):
    t, d = x.shape
    pipelined = _nbytes((tm, d), F32) + 2 * _nbytes((d, tn), BF16) + _nbytes((tm, tn), F32)
    resident = _nbytes((tm, d), BF16) + 2 * _nbytes((tm, tn), F32)
    return pl.pallas_call(
        _glu_kernel,
        out_shape=jax.ShapeDtypeStruct((t, width), F32),
        grid=(t // tm, width // tn),
        in_specs=[
            pl.BlockSpec((tm, d), lambda i, j: (i, 0)),
            pl.BlockSpec((1, d), lambda i, j: (0, 0)),
            _wspec(layer, (d, tn), lambda i, j: (0, a_col0 // tn + j)),
            _wspec(layer, (d, tn), lambda i, j: (0, g_col0 // tn + j)),
        ],
        out_specs=pl.BlockSpec((tm, tn), lambda i, j: (i, j)),
        scratch_shapes=[pltpu.VMEM((tm, d), BF16)],
        compiler_params=_params(("parallel", "arbitrary"), pipelined, resident),
        name="glu_proj",
    )(x, gain.reshape(1, d), w_in, w_in)


N_ROW_OFFSETS = 2 * WIN_ROWS - 1
MASKED_ROW = N_ROW_OFFSETS


def _attn_bias_rows(rpb):
    n_heads = rpb.shape[0]
    pad = GRID_W - WIN_COLS
    padded = jnp.pad(rpb.astype(F32), ((0, 0), (0, 0), (pad, pad)))
    rows = jnp.stack([padded[:, :, GRID_W - 1 - c:2 * GRID_W - 1 - c] for c in range(GRID_W)], axis=2)
    c = np.arange(GRID_W)[:, None]
    kc = np.arange(GRID_W)[None, :]
    c0 = np.clip(c - WIN_COLS // 2, 0, GRID_W - WIN_COLS)
    c_ok = (kc >= c0) & (kc < c0 + WIN_COLS)
    rows = jnp.where(c_ok[None, None], rows, MASKED)
    rows = jnp.concatenate([rows, jnp.full((n_heads, 1, GRID_W, GRID_W), MASKED, F32)], axis=1)
    return jnp.concatenate([rows, rows], axis=3)


def _attn_kernel(lo_of, hi_of, q_ref, k0_ref, k1_ref, k2_ref, v0_ref, v1_ref, v2_ref, t_ref, o_ref, b_ref):
    i = pl.program_id(1)
    lo = lo_of(i)
    hi = hi_of(i)
    n_pairs = q_ref.shape[0]
    lane = lax.broadcasted_iota(jnp.int32, (GRID_W, LANES), 1)

    @pl.when((i == lo) | (i == lo + 1) | (i == hi - 1))
    def _():
        case = jnp.where(i == lo, 0, jnp.where(i == hi - 1, 2, 1))
        for qr in range(Q_ROWS):
            first = jnp.where(case == 0, 0, jnp.where(case == 1, qr, Q_ROWS))

            def entry(j):
                ok = (j >= first) & (j < first + WIN_ROWS)
                return jnp.where(ok, j - case * Q_ROWS - qr + (WIN_ROWS - 1), MASKED_ROW)

            for h in range(2 * n_pairs):
                r0 = (h % 2) * Q_TOK + qr * GRID_W
                for j in range(0, K_ROWS, 2):
                    left = t_ref[h, entry(j)]
                    right = t_ref[h, entry(j + 1)]
                    b_ref[h // 2, r0:r0 + GRID_W, j * GRID_W:(j + 2) * GRID_W] = jnp.where(lane < GRID_W, left, right)

    qlane = lax.broadcasted_iota(jnp.int32, (Q_TOK, LANES), 1)
    for p in range(n_pairs):
        q = q_ref[p]
        zero = jnp.zeros_like(q)
        q2 = jnp.concatenate([jnp.where(qlane < HEAD_DIM, q, zero), jnp.where(qlane >= HEAD_DIM, q, zero)], axis=0)
        k = jnp.concatenate([k0_ref[p], k1_ref[p], k2_ref[p]], axis=0)
        v = jnp.concatenate([v0_ref[p], v1_ref[p], v2_ref[p]], axis=0)
        s = lax.dot_general(q2, k, (((1,), (1,)), ((), ())), preferred_element_type=F32)
        s = s * (HEAD_DIM ** -0.5) + b_ref[p]
        e = jnp.exp(s - jnp.max(s, axis=-1, keepdims=True))
        denom = jnp.sum(e, axis=-1, keepdims=True)
        o2 = jnp.dot(e.astype(BF16), v, preferred_element_type=F32) / denom
        o_ref[p] = jnp.where(qlane < HEAD_DIM, o2[:Q_TOK], o2[Q_TOK:]).astype(BF16)


def _attention(qkv, bias_rows, seg_groups, *, pairs_per_step):
    n_pairs = qkv.shape[0] // 3
    t = qkv.shape[1]
    pb = pairs_per_step
    assert n_pairs % pb == 0
    lo_of = lambda i: _seg_select(i, seg_groups, seg_groups[:-1])
    hi_of = lambda i: _seg_select(i, seg_groups, seg_groups[1:])
    window = lambda i: jnp.clip(i - 1, lo_of(i), hi_of(i) - K_ROWS // Q_ROWS)

    blk = (pb, Q_TOK, LANES)
    kv_specs = [pl.BlockSpec(blk, functools.partial(lambda p, i, off, j: (off + p, window(i) + j, 0), off=off, j=j))
                for off in (n_pairs // pb, 2 * n_pairs // pb) for j in range(K_ROWS // Q_ROWS)]
    tab_blk = (2 * pb,) + bias_rows.shape[1:]
    pipelined = 8 * _nbytes(blk, BF16) + _nbytes(tab_blk, F32)
    resident = _nbytes((pb, 2 * Q_TOK, K_TOK), F32) + 3 * pb * _nbytes((2 * Q_TOK, K_TOK), F32)
    return pl.pallas_call(
        functools.partial(_attn_kernel, lo_of, hi_of),
        out_shape=jax.ShapeDtypeStruct((n_pairs, t, LANES), BF16),
        grid=(n_pairs // pb, t // Q_TOK),
        in_specs=[pl.BlockSpec(blk, lambda p, i: (p, i, 0))] + kv_specs + [
            pl.BlockSpec(tab_blk, lambda p, i: (p, 0, 0, 0))],
        out_specs=pl.BlockSpec(blk, lambda p, i: (p, i, 0)),
        scratch_shapes=[pltpu.VMEM((pb, 2 * Q_TOK, K_TOK), F32)],
        compiler_params=_params(("arbitrary", "arbitrary"), pipelined, resident),
        name="nbr_attention",
    )(qkv, *([qkv] * 6), bias_rows)


def _conv_kernel(first_tiles, last_tiles, prev_ref, cur_ref, next_ref, w_ref, cb_ref, lg_ref, lb_ref,
                 o_ref, buf_ref, sh_ref, y_ref):
    i = pl.program_id(0)
    tm, c = cur_ref.shape
    is_first = functools.reduce(jnp.logical_or, [i == s for s in first_tiles])
    is_last = functools.reduce(jnp.logical_or, [i == s for s in last_tiles])
    buf_ref[0:CONV_HALO, :] = jnp.where(is_first, 0.0, prev_ref[...])
    buf_ref[CONV_HALO:CONV_HALO + tm, :] = cur_ref[...]
    buf_ref[CONV_HALO + tm:, :] = jnp.where(is_last, 0.0, next_ref[...])
    chunk = 128
    base = CONV_HALO - CONV_K // 2
    n_sh = sh_ref.shape[1]
    for l0 in range(0, c, LANES):
        lanes = slice(l0, l0 + LANES)
        for o in range(1, F32_SUBLANES):
            sh_ref[o] = buf_ref[o:o + n_sh, lanes]
        def chunk_body(ci, carry, lanes=lanes):
            t0 = pl.multiple_of(ci * chunk, chunk)
            acc = jnp.broadcast_to(cb_ref[:, lanes], (chunk, LANES))
            for k in range(CONV_K):
                o = (base + k) % F32_SUBLANES
                rows = pl.ds(t0 + (base + k - o), chunk)
                src = buf_ref[rows, lanes] if o == 0 else sh_ref[o, rows, :]
                acc = acc + w_ref[k:k + 1, lanes] * src
            y_ref[pl.ds(t0, chunk), lanes] = acc
            return carry

        lax.fori_loop(0, tm // chunk, chunk_body, 0)
    y = y_ref[...]
    mu = jnp.mean(y, axis=-1, keepdims=True)
    yc = y - mu
    var = jnp.mean(yc * yc, axis=-1, keepdims=True)
    z = yc * lax.rsqrt(var + NORM_EPS) * lg_ref[...] + lb_ref[...]
    o_ref[...] = (z * jax.nn.sigmoid(z)).astype(BF16)


def _conv_module(c, conv_w, conv_b, ln_g, ln_b, seq_starts, seq_ends, *, tm):
    t, width = c.shape
    first_tiles = tuple(s // tm for s in seq_starts)
    last_tiles = tuple(e // tm - 1 for e in seq_ends)
    hb = tm // CONV_HALO
    n_hb = t // CONV_HALO
    pipelined = _nbytes((tm + 2 * CONV_HALO, width), F32) + _nbytes((tm, width), BF16)
    resident = _nbytes((2 * tm + 2 * CONV_HALO, width), F32) + 3 * _nbytes((tm, width), F32)
    row = lambda a: a.reshape(1, width)
    return pl.pallas_call(
        functools.partial(_conv_kernel, first_tiles, last_tiles),
        out_shape=jax.ShapeDtypeStruct((t, width), BF16),
        grid=(t // tm,),
        in_specs=[
            pl.BlockSpec((CONV_HALO, width), lambda i: (jnp.maximum(i * hb - 1, 0), 0)),
            pl.BlockSpec((tm, width), lambda i: (i, 0)),
            pl.BlockSpec((CONV_HALO, width), lambda i: (jnp.minimum((i + 1) * hb, n_hb - 1), 0)),
            pl.BlockSpec((CONV_K, width), lambda i: (0, 0)),
            pl.BlockSpec((1, width), lambda i: (0, 0)),
            pl.BlockSpec((1, width), lambda i: (0, 0)),
            pl.BlockSpec((1, width), lambda i: (0, 0)),
        ],
        out_specs=pl.BlockSpec((tm, width), lambda i: (i, 0)),
        scratch_shapes=[pltpu.VMEM((tm + 2 * CONV_HALO, width), F32),
                        pltpu.VMEM((F32_SUBLANES, tm + 2 * CONV_HALO - F32_SUBLANES, LANES), F32),
                        pltpu.VMEM((tm, width), F32)],
        compiler_params=_params(("parallel",), pipelined, resident),
        name="conv_module",
    )(c, c, c, conv_w, row(conv_b), row(ln_g), row(ln_b))


def _outproj_kernel(x_ref, att_ref, cc_ref, w1_ref, w2_ref, o_ref):
    att = jnp.concatenate([att_ref[p] for p in range(att_ref.shape[0])], axis=1)
    acc = jnp.dot(att, w1_ref[...], preferred_element_type=F32)
    acc = acc + jnp.dot(cc_ref[...], w2_ref[...], preferred_element_type=F32)
    o_ref[...] = x_ref[...] + acc


def _outproj(x, att, cc, w_out, layer, *, tm, tn):
    t, d = x.shape
    n_pairs = att.shape[0]
    wa = n_pairs * LANES
    wb = cc.shape[1]
    assert wa == wb and wa + wb == w_out.shape[1]
    pipelined = (2 * _nbytes((tm, tn), F32) + _nbytes((tm, wa), BF16) + _nbytes((tm, wb), BF16)
                 + _nbytes((wa + wb, tn), BF16))
    resident = 2 * _nbytes((tm, tn), F32) + _nbytes((tm, wa), BF16)
    return pl.pallas_call(
        _outproj_kernel,
        out_shape=jax.ShapeDtypeStruct((t, d), F32),
        grid=(t // tm, d // tn),
        in_specs=[
            pl.BlockSpec((tm, tn), lambda i, j: (i, j)),
            pl.BlockSpec((n_pairs, tm, LANES), lambda i, j: (0, i, 0)),
            pl.BlockSpec((tm, wb), lambda i, j: (i, 0)),
            _wspec(layer, (wa, tn), lambda i, j: (0, j)),
            _wspec(layer, (wb, tn), lambda i, j: (1, j)),
        ],
        out_specs=pl.BlockSpec((tm, tn), lambda i, j: (i, j)),
        compiler_params=_params(("parallel", "arbitrary"), pipelined, resident),
        name="attn_conv_out_proj",
    )(x, att, cc, w_out, w_out)


def _unit_circle(num, den):
    ang = (num % den).astype(F32) * F32(2.0 * np.pi / den)
    return jnp.cos(ang), jnp.sin(ang)


def _dft_cos_sin(n):
    idx = jnp.arange(n, dtype=jnp.int32)
    return _unit_circle(idx[:, None] * idx[None, :], n)


def _deinterleave_kernel(x_ref, g_ref, o_ref, hs_ref):
    tm, d = x_ref.shape
    radix = o_ref.shape[1]
    h = _rms(x_ref[...], g_ref[...])
    for s in range(d // LANES):
        hs_ref[s] = h[:, s * LANES:(s + 1) * LANES]
    for n1 in range(radix):
        for s in range(d // LANES):
            o_ref[0, n1, :, s * LANES:(s + 1) * LANES] = hs_ref[s, pl.ds(n1, tm // radix, stride=radix), :].astype(BF16)


def _deinterleave(x, gain, tok0, n_seq, s, *, tm):
    d = x.shape[1]
    tiles = s // tm
    blk0 = tok0 // tm
    assert blk0 * tm == tok0 and tiles * tm == s and (tm // DFT_RADIX) % BF16_SUBLANES == 0
    pipelined = _nbytes((tm, d), F32) + _nbytes((tm, d), BF16)
    resident = 2 * _nbytes((tm, d), F32)
    return pl.pallas_call(
        _deinterleave_kernel,
        out_shape=jax.ShapeDtypeStruct((n_seq, DFT_RADIX, s // DFT_RADIX, d), BF16),
        grid=(n_seq, tiles),
        in_specs=[pl.BlockSpec((tm, d), lambda b, i: (blk0 + b * tiles + i, 0)),
                  pl.BlockSpec((1, d), lambda b, i: (0, 0))],
        out_specs=pl.BlockSpec((1, DFT_RADIX, tm // DFT_RADIX, d), lambda b, i: (b, 0, i, 0)),
        scratch_shapes=[pltpu.VMEM((d // LANES, tm, LANES), F32)],
        compiler_params=_params(("parallel", "parallel"), pipelined, resident),
        name="fourier_deinterleave",
    )(x, gain.reshape(1, d))


def _dft_dense_kernel(fc_ref, fs_ref, tw_ref, h_ref, o_ref):
    h = h_ref[0, 0]
    yc = jnp.dot(fc_ref[...], h, preferred_element_type=F32)
    ys = jnp.dot(fs_ref[...], h, preferred_element_type=F32)
    tc = tw_ref[0, :, 0:1]
    ts = tw_ref[0, :, 1:2]
    o_ref[0, 0, 0] = (tc * yc - ts * ys).astype(BF16)
    o_ref[0, 0, 1] = (-(tc * ys + ts * yc)).astype(BF16)


def _dft_dense(h, fc, fs, tw, *, tr, tc):
    n_seq, radix, s2, d = h.shape
    pipelined = 2 * _nbytes((tr, s2), BF16) + _nbytes((s2, tc), BF16) + _nbytes((2, tr, tc), BF16)
    resident = 4 * _nbytes((tr, tc), F32)
    return pl.pallas_call(
        _dft_dense_kernel,
        out_shape=jax.ShapeDtypeStruct((n_seq, radix, 2, s2, d), BF16),
        grid=(n_seq, radix, d // tc, s2 // tr),
        in_specs=[
            pl.BlockSpec((tr, s2), lambda b, n, c, r: (r, 0)),
            pl.BlockSpec((tr, s2), lambda b, n, c, r: (r, 0)),
            pl.BlockSpec((1, tr, 2), lambda b, n, c, r: (n, r, 0)),
            pl.BlockSpec((1, 1, s2, tc), lambda b, n, c, r: (b, n, 0, c)),
        ],
        out_specs=pl.BlockSpec((1, 1, 2, tr, tc), lambda b, n, c, r: (b, n, 0, r, c)),
        compiler_params=_params(("parallel", "parallel", "parallel", "arbitrary"), pipelined, resident),
        name="fourier_dense_dft",
    )(fc, fs, tw, h)


def _fft(z):
    n = len(z)
    if n == 1:
        return z
    even, odd = _fft(z[0::2]), _fft(z[1::2])
    out = [None] * n
    for k in range(n // 2):
        o_r, o_i = odd[k]
        if k == 0:
            t_r, t_i = o_r, o_i
        elif 4 * k == n:
            t_r, t_i = o_i, -o_r
        else:
            c, s = math.cos(2 * math.pi * k / n), math.sin(2 * math.pi * k / n)
            t_r, t_i = c * o_r + s * o_i, c * o_i - s * o_r
        e_r, e_i = even[k]
        out[k] = (e_r + t_r, e_i + t_i)
        out[k + n // 2] = (e_r - t_r, e_i - t_i)
    return out


def _radix_kernel(z_ref, pr_ref, pi_ref):
    radix, tq, tc = z_ref.shape[1], z_ref.shape[3], z_ref.shape[4]

    def body(g, carry):
        rows = pl.ds(pl.multiple_of(g * BF16_SUBLANES, BF16_SUBLANES), BF16_SUBLANES)
        for l0 in range(0, tc, LANES):
            lanes = slice(l0, l0 + LANES)
            z = [(z_ref[0, n1, 0, rows, lanes].astype(F32), z_ref[0, n1, 1, rows, lanes].astype(F32))
                 for n1 in range(radix)]
            for k1, (p_r, p_i) in enumerate(_fft(z)):
                pr_ref[0, k1, rows, lanes] = p_r.astype(BF16)
                pi_ref[0, k1, rows, lanes] = p_i.astype(BF16)
        return carry

    lax.fori_loop(0, tq // BF16_SUBLANES, body, 0)


def _radix_combine(z, *, tq, tc):
    n_seq, radix, _, s2, d = z.shape
    out = jax.ShapeDtypeStruct((n_seq, radix, s2, d), BF16)
    oblk = pl.BlockSpec((1, radix, tq, tc), lambda b, q, c: (b, 0, q, c))
    pipelined = 2 * _nbytes((radix, 2, tq, tc), BF16)
    return pl.pallas_call(
        _radix_kernel,
        out_shape=(out, out),
        grid=(n_seq, s2 // tq, d // tc),
        in_specs=[pl.BlockSpec((1, radix, 2, tq, tc), lambda b, q, c: (b, 0, 0, q, c))],
        out_specs=(oblk, oblk),
        compiler_params=_params(("parallel", "parallel", "parallel"), pipelined),
        name="fourier_radix_combine",
    )(z)


def _fourier_out_kernel(scale, x_ref, pr_ref, pi_ref, d_ref, w_ref, o_ref, y_ref):
    @pl.when(pl.program_id(1) == 0)
    def _():
        gw = d_ref.shape[1]
        for q in range(pr_ref.shape[1] // gw):
            cols = slice(q * gw, (q + 1) * gw)
            lhs = jnp.concatenate([pr_ref[:, cols], pi_ref[:, cols]], axis=1)
            y_ref[:, cols] = (jnp.dot(lhs, d_ref[...], preferred_element_type=F32) * scale).astype(BF16)

    o_ref[...] = x_ref[...] + jnp.dot(y_ref[...], w_ref[...], preferred_element_type=F32)


def _fourier_out(x, p_r, p_i, dmat, w, layer, scale, tok0, *, tm, tn):
    t, d = x.shape
    n_tok = p_r.shape[0]
    blk0 = tok0 // tm
    assert blk0 * tm == tok0 and n_tok % tm == 0
    pipelined = (2 * _nbytes((tm, tn), F32) + 2 * _nbytes((tm, d), BF16) + _nbytes(dmat.shape, BF16)
                 + _nbytes((d, tn), BF16))
    resident = _nbytes((tm, d), BF16) + 2 * _nbytes((tm, tn), F32)
    return pl.pallas_call(
        functools.partial(_fourier_out_kernel, scale),
        out_shape=jax.ShapeDtypeStruct((t, d), F32),
        grid=(n_tok // tm, d // tn),
        in_specs=[
            pl.BlockSpec((tm, tn), lambda i, j: (blk0 + i, j)),
            pl.BlockSpec((tm, d), lambda i, j: (i, 0)),
            pl.BlockSpec((tm, d), lambda i, j: (i, 0)),
            pl.BlockSpec(dmat.shape, lambda i, j: (0, 0)),
            _wspec(layer, (d, tn), lambda i, j: (0, j)),
        ],
        out_specs=pl.BlockSpec((tm, tn), lambda i, j: (blk0 + i, j)),
        scratch_shapes=[pltpu.VMEM((tm, d), BF16)],
        input_output_aliases={0: 0},
        compiler_params=_params(("parallel", "arbitrary"), pipelined, resident),
        name="fourier_out_proj",
    )(x, p_r, p_i, dmat, w)


def _fourier_mix(x, gain, w_out, layer, groups, *, tm):
    d = x.shape[1]
    gw = d // FOURIER_GROUPS
    cc, sc = _dft_cos_sin(gw)
    dmat = jnp.concatenate([cc, sc], axis=0).astype(BF16)
    mixed = []
    for tok0, n_seq, s in groups:
        s2 = s // DFT_RADIX
        fc, fs = _dft_cos_sin(s2)
        n1 = jnp.arange(DFT_RADIX, dtype=jnp.int32)[:, None]
        k2 = jnp.arange(s2, dtype=jnp.int32)[None, :]
        tw = jnp.stack(_unit_circle(n1 * k2, s), axis=2)
        h = _deinterleave(x, gain, tok0, n_seq, s, tm=tm)
        z = _dft_dense(h, fc.astype(BF16), fs.astype(BF16), tw, tr=min(s2, 1024), tc=_tile(d, 1024))
        p_r, p_i = _radix_combine(z, tq=min(s2, 2 * BF16_SUBLANES), tc=d)
        mixed.append((p_r.reshape(n_seq * s, d), p_i.reshape(n_seq * s, d), float((s * gw) ** -0.5), tok0))
    for p_r, p_i, scale, tok0 in mixed:
        x = _fourier_out(x, p_r, p_i, dmat, w_out, layer, scale, tok0, tm=tm // 2, tn=d)
    return x


def _trunk(parts, seqs, out_ranges, p):
    d = parts[0].shape[1]
    t = sum(a.shape[0] for a in parts)
    tm = _tile(math.gcd(*[n for _, n in seqs]), 1024)
    depth = p["ffn1_norm"].shape[0]
    a_width = p["ab_rpb"].shape[1] * HEAD_DIM
    b_width = d - a_width
    seq_starts = [s for s, _ in seqs]
    seq_ends = [s + n for s, n in seqs]
    seg_groups = [s // Q_TOK for s in seq_starts] + [seq_ends[-1] // Q_TOK]
    for s, n in seqs:
        assert s % tm == 0 and n % tm == 0 and n >= K_TOK
    groups = []
    for s, n in seqs:
        if groups and groups[-1][2] == n and groups[-1][0] + groups[-1][1] * n == s:
            groups[-1] = (groups[-1][0], groups[-1][1] + 1, n)
        else:
            groups.append((s, 1, n))
    w = {name: _cast_bf16(p[name]) for name in (
        "ffn1_w_gate", "ffn1_w_up", "ffn1_w_down", "ffn2_w_gate", "ffn2_w_up", "ffn2_w_down",
        "ab_w_in", "ab_w_out", "c_w_out")}
    tm_ffn, tf = tm, _ffn_hidden_tile(tm, d, p["ffn1_w_gate"].shape[2])
    x = jnp.zeros((t, d), F32)
    for i in range(depth):
        ffn1 = functools.partial(_ffn, gain=p["ffn1_norm"][i], w_gate=w["ffn1_w_gate"], w_up=w["ffn1_w_up"],
                                 w_down=w["ffn1_w_down"], layer=i, tm=tm_ffn, tf=tf)
        if i == 0:
            tok0 = 0
            for part in parts:
                x = ffn1(part, out_tok0=tok0, out_total=t, out_buf=x)
                tok0 += part.shape[0]
        else:
            x = ffn1(x)
        j = i // 2
        if i % 2 == 0:
            qkv = _qkv(x, p["mix_norm"][i], w["ab_w_in"], j, 3 * a_width, tm=tm, tn=_tile(a_width, 1024))
            c = _glu(x, p["mix_norm"][i], w["ab_w_in"], j, 3 * a_width, 3 * a_width + b_width, b_width,
                     tm=tm, tn=_tile(b_width, 1024))
            att = _attention(qkv, _attn_bias_rows(p["ab_rpb"][j]), seg_groups,
                             pairs_per_step=math.gcd(a_width // LANES, 8))
            cc = _conv_module(c, p["ab_conv_w"][j], p["ab_conv_b"][j], p["ab_ln_g"][j], p["ab_ln_b"][j],
                              seq_starts, seq_ends, tm=min(tm, 512))
            x = _outproj(x, att, cc, w["ab_w_out"], j, tm=tm // 2, tn=d)
        else:
            x = _fourier_mix(x, p["mix_norm"][i], w["c_w_out"], j, groups, tm=tm)
        ffn2 = functools.partial(_ffn, x, p["ffn2_norm"][i], w["ffn2_w_gate"], w["ffn2_w_up"], w["ffn2_w_down"], i,
                                 tm=tm_ffn, tf=tf)
        if i < depth - 1:
            x = ffn2()
    return [ffn2(out_gain=p["final_norm"], tok0=tok0, n_tok=n_tok) for tok0, n_tok in out_ranges]


def kernel(x_prompt, x_sample, ffn1_norm, ffn1_w_gate, ffn1_w_up, ffn1_w_down, mix_norm, ab_w_in, ab_rpb, ab_conv_w, ab_conv_b, ab_ln_g, ab_ln_b, ab_w_out, c_w_out, ffn2_norm, ffn2_w_gate, ffn2_w_up, ffn2_w_down, final_norm):
    params = dict(ffn1_norm=ffn1_norm, ffn1_w_gate=ffn1_w_gate, ffn1_w_up=ffn1_w_up, ffn1_w_down=ffn1_w_down,
                  mix_norm=mix_norm, ab_w_in=ab_w_in, ab_rpb=ab_rpb, ab_conv_w=ab_conv_w, ab_conv_b=ab_conv_b,
                  ab_ln_g=ab_ln_g, ab_ln_b=ab_ln_b, ab_w_out=ab_w_out, c_w_out=c_w_out, ffn2_norm=ffn2_norm,
                  ffn2_w_gate=ffn2_w_gate, ffn2_w_up=ffn2_w_up, ffn2_w_down=ffn2_w_down, final_norm=final_norm)
    bp, sp, d = x_prompt.shape
    bs, ss, _ = x_sample.shape
    n_s, n_p = bs * ss, bp * sp
    parts = [x_sample.reshape(n_s, d), x_prompt.reshape(n_p, d)]
    seqs = [(b * ss, ss) for b in range(bs)] + [(n_s + b * sp, sp) for b in range(bp)]
    y_sample, y_prompt = _trunk(parts, seqs, [(0, n_s), (n_s, n_p)], params)
    return (y_prompt.reshape(bp, sp, d), y_sample.reshape(bs, ss, d))
```
